```python
import math
import jax, jax.numpy as jnp
from jax import lax
import numpy as np

D_MODEL = 1024
BATCH = 2
SEQ = 16384
DEPTH = 2

N_HEADS_ATTN = 8
HEAD_DIM = 64
V_DIM = 2 * HEAD_DIM
QK_WIDTH = N_HEADS_ATTN * 2 * HEAD_DIM
ATTN_WIDTH = N_HEADS_ATTN * V_DIM
Q_BLOCK = 128
N_BUCKETS = 32
MAX_DISTANCE = 128
CONV_WIDTH = D_MODEL
CONV_K = 3
IN_SPLITS = (QK_WIDTH, 2 * QK_WIDTH, 2 * QK_WIDTH + ATTN_WIDTH,
             2 * QK_WIDTH + ATTN_WIDTH + CONV_WIDTH,
             2 * QK_WIDTH + ATTN_WIDTH + 2 * CONV_WIDTH,
             2 * QK_WIDTH + ATTN_WIDTH + 3 * CONV_WIDTH,
             2 * QK_WIDTH + ATTN_WIDTH + 3 * CONV_WIDTH + D_MODEL)
IN_COLS = 2 * QK_WIDTH + ATTN_WIDTH + 3 * CONV_WIDTH + 2 * D_MODEL
N_GROUPS = 4
EXPERTS_PER_GROUP = 8
N_EXPERTS = N_GROUPS * EXPERTS_PER_GROUP
TOP_K_IN_GROUP = 2
D_EXPERT = 512
EXPERT_BLOCK = 128
LN_EPS = 1e-5
DN_ALPHA = (2 * DEPTH) ** 0.25
DN_BETA = (8 * DEPTH) ** -0.25

kernel_name = "hybrid_diffattn_shortconv_hmoe_deepnorm"


def layer_norm(x, g, b):
    x32 = x.astype(jnp.float32)
    mu = jnp.mean(x32, axis=-1, keepdims=True)
    var = jnp.mean(jnp.square(x32 - mu), axis=-1, keepdims=True)
    return ((x32 - mu) * lax.rsqrt(var + LN_EPS)).astype(x.dtype) * g + b


def rms_norm(x, g):
    x32 = x.astype(jnp.float32)
    ms = jnp.mean(jnp.square(x32), axis=-1, keepdims=True)
    return (x32 * lax.rsqrt(ms + LN_EPS)).astype(x.dtype) * g


def rel_bucket(rel):
    n = jnp.maximum(rel, 0)
    max_exact = N_BUCKETS // 2
    nf = jnp.maximum(n, 1).astype(jnp.float32)
    large = max_exact + (jnp.log(nf / max_exact) / math.log(MAX_DISTANCE / max_exact)
                         * (N_BUCKETS - max_exact)).astype(jnp.int32)
    large = jnp.minimum(large, N_BUCKETS - 1)
    return jnp.where(n < max_exact, n, large)


def diff_attention(q, k, v, lam, rel_table):
    B, S = q.shape[:2]
    nb = S // Q_BLOCK
    scale = HEAD_DIM ** -0.5
    k1, k2 = k[:, :, :, 0], k[:, :, :, 1]
    kpos = jnp.arange(S, dtype=jnp.int32)
    qb = q.reshape(B, nb, Q_BLOCK, N_HEADS_ATTN, 2, HEAD_DIM).transpose(1, 0, 2, 3, 4, 5)
    starts = jnp.arange(nb, dtype=jnp.int32) * Q_BLOCK

    def block(args):
        qblk, start = args
        qpos = start + jnp.arange(Q_BLOCK, dtype=jnp.int32)
        rel = qpos[:, None] - kpos[None, :]
        bias = jnp.transpose(rel_table[rel_bucket(rel)], (2, 0, 1)).astype(jnp.float32)
        causal = rel >= 0

        def probs(qh, kh):
            s = jnp.einsum('bqhd,bkhd->bhqk', qh, kh).astype(jnp.float32) * scale + bias
            return jax.nn.softmax(jnp.where(causal, s, -jnp.inf), axis=-1)

        a = probs(qblk[:, :, :, 0], k1) - lam * probs(qblk[:, :, :, 1], k2)
        return jnp.einsum('bhqk,bkhe->bqhe', a.astype(v.dtype), v)

    out = lax.map(block, (qb, starts))
    return out.transpose(1, 0, 2, 3, 4).reshape(B, S, N_HEADS_ATTN, V_DIM)


def short_conv(u, w):
    return lax.conv_general_dilated(u, w[:, None, :], window_strides=(1,),
                                    padding=((CONV_K - 1, 0),),
                                    dimension_numbers=('NWC', 'WIO', 'NWC'),
                                    feature_group_count=u.shape[-1])


def mixer(h, layer, w_in, gate_b, lam_vecs, subln_g, w_attn_proj, conv_w, w_conv_proj, w_out, rel_table):
    B, S, _ = h.shape
    z = h @ w_in
    q, k, v, cb, cc, cx, ga, gc = jnp.split(z, IN_SPLITS, axis=-1)
    lam_init = 0.8 - 0.6 * math.exp(-0.3 * layer)
    lv = lam_vecs.astype(jnp.float32)
    lam = jnp.exp(jnp.sum(lv[0] * lv[1])) - jnp.exp(jnp.sum(lv[2] * lv[3])) + lam_init
    o = diff_attention(q.reshape(B, S, N_HEADS_ATTN, 2, HEAD_DIM),
                       k.reshape(B, S, N_HEADS_ATTN, 2, HEAD_DIM),
                       v.reshape(B, S, N_HEADS_ATTN, V_DIM), lam, rel_table)
    o = (rms_norm(o, subln_g) * (1.0 - lam_init)).reshape(B, S, ATTN_WIDTH) @ w_attn_proj
    yc = (cb * short_conv(cc * cx, conv_w)) @ w_conv_proj
    m = jax.nn.sigmoid(ga + gate_b[0]) * o + jax.nn.sigmoid(gc + gate_b[1]) * yc
    return m @ w_out


def hier_moe(x, w_rg, b_rg, w_re, b_re, w1, w3, w2):
    B, S, D = x.shape
    n_tok = B * S
    n_asg = n_tok * TOP_K_IN_GROUP
    xt = x.reshape(n_tok, D)
    g_prob = jax.nn.softmax((xt @ w_rg + b_rg).astype(jnp.float32), axis=-1)
    g_p, g_idx = lax.top_k(g_prob, 1)
    e_logits = (xt @ w_re + b_re).astype(jnp.float32).reshape(n_tok, N_GROUPS, EXPERTS_PER_GROUP)
    e_logits = jnp.take_along_axis(e_logits, g_idx[:, :, None], axis=1)[:, 0]
    e_p, e_loc = lax.top_k(jax.nn.softmax(e_logits, axis=-1), TOP_K_IN_GROUP)
    gate = g_p * e_p / jnp.sum(e_p, axis=-1, keepdims=True)
    expert = (g_idx * EXPERTS_PER_GROUP + e_loc).reshape(-1)
    order = jnp.argsort(expert)
    e_sorted = expert[order]
    tok = order // TOP_K_IN_GROUP
    sizes = jnp.bincount(expert, length=N_EXPERTS)
    padded = (sizes + EXPERT_BLOCK - 1) // EXPERT_BLOCK * EXPERT_BLOCK
    pad_end = jnp.cumsum(padded)
    pad_start = pad_end - padded
    start = jnp.cumsum(sizes) - sizes
    dest = pad_start[e_sorted] + jnp.arange(n_asg) - start[e_sorted]
    n_blk = (n_asg + N_EXPERTS * (EXPERT_BLOCK - 1) + EXPERT_BLOCK - 1) // EXPERT_BLOCK
    xbuf = jnp.zeros((n_blk * EXPERT_BLOCK, D), x.dtype).at[dest].set(xt[tok])
    blk_e = jnp.minimum(jnp.searchsorted(pad_end, jnp.arange(n_blk) * EXPERT_BLOCK, side='right'),
                        N_EXPERTS - 1)

    def expert_block(args):
        xb, e = args
        return (jax.nn.silu(xb @ w1[e]) * (xb @ w3[e])) @ w2[e]

    ybuf = lax.map(expert_block, (xbuf.reshape(n_blk, EXPERT_BLOCK, D), blk_e)).reshape(-1, D)
    y = ybuf[dest] * gate.reshape(-1)[order][:, None].astype(x.dtype)
    return jnp.zeros_like(xt).at[tok].add(y).reshape(B, S, D)


def setup_inputs(seed: int = 0) -> dict:
    key = jax.random.key(seed)
    ks = jax.random.split(key, 24)

    def nrm(k, shape, scale):
        return jax.random.normal(k, shape, jnp.float32) * scale

    L, D = DEPTH, D_MODEL
    return {
        "x": nrm(ks[0], (BATCH, SEQ, D), 1.0),
        "ln0_g": 1.0 + nrm(ks[1], (D,), 0.02),
        "ln0_b": nrm(ks[2], (D,), 0.02),
        "rel_table": nrm(ks[3], (N_BUCKETS, N_HEADS_ATTN), 0.5),
        "w_in": nrm(ks[4], (L, D, IN_COLS), D ** -0.5),
        "gate_b": nrm(ks[5], (L, 2, D), 0.1),
        "lam_vecs": nrm(ks[6], (L, 4, HEAD_DIM), 0.1),
        "subln_g": 1.0 + nrm(ks[7], (L, V_DIM), 0.02),
        "w_attn_proj": nrm(ks[8], (L, ATTN_WIDTH, D), ATTN_WIDTH ** -0.5),
        "conv_w": nrm(ks[9], (L, CONV_K, CONV_WIDTH), CONV_K ** -0.5),
        "w_conv_proj": nrm(ks[10], (L, CONV_WIDTH, D), CONV_WIDTH ** -0.5),
        "w_out": nrm(ks[11], (L, D, D), D ** -0.5 * DN_BETA),
        "ln1_g": 1.0 + nrm(ks[12], (L, D), 0.02),
        "ln1_b": nrm(ks[13], (L, D), 0.02),
        "w_rg": nrm(ks[14], (L, D, N_GROUPS), D ** -0.5),
        "b_rg": nrm(ks[15], (L, N_GROUPS), 0.01),
        "w_re": nrm(ks[16], (L, D, N_EXPERTS), D ** -0.5),
        "b_re": nrm(ks[17], (L, N_EXPERTS), 0.01),
        "w1": nrm(ks[18], (L, N_EXPERTS, D, D_EXPERT), D ** -0.5),
        "w3": nrm(ks[19], (L, N_EXPERTS, D, D_EXPERT), D ** -0.5),
        "w2": nrm(ks[20], (L, N_EXPERTS, D_EXPERT, D), D_EXPERT ** -0.5 * DN_BETA),
        "ln2_g": 1.0 + nrm(ks[21], (L, D), 0.02),
        "ln2_b": nrm(ks[22], (L, D), 0.02),
    }


def reference(x, ln0_g, ln0_b, rel_table, w_in, gate_b, lam_vecs, subln_g, w_attn_proj, conv_w,
              w_conv_proj, w_out, ln1_g, ln1_b, w_rg, b_rg, w_re, b_re, w1, w3, w2, ln2_g, ln2_b):
    h = layer_norm(x, ln0_g, ln0_b)
    for l in range(DEPTH):
        mix = mixer(h, l, w_in[l], gate_b[l], lam_vecs[l], subln_g[l], w_attn_proj[l],
                    conv_w[l], w_conv_proj[l], w_out[l], rel_table)
        h = layer_norm(DN_ALPHA * h + mix, ln1_g[l], ln1_b[l])
        ffn = hier_moe(h, w_rg[l], b_rg[l], w_re[l], b_re[l], w1[l], w3[l], w2[l])
        h = layer_norm(DN_ALPHA * h + ffn, ln2_g[l], ln2_b[l])
    return h
```

```python
import functools
import math

import numpy as np
import jax
import jax.numpy as jnp
from jax import lax
from jax.experimental import pallas as pl
from jax.experimental.pallas import tpu as pltpu

F32 = jnp.float32
BF16 = jnp.bfloat16

N_HEADS = 8
HEAD_DIM = 64
V_DIM = 2 * HEAD_DIM
N_BUCKETS = 32
MAX_DISTANCE = 128
CONV_K = 3
N_GROUPS = 4
EXPERTS_PER_GROUP = 8
N_EXPERTS = N_GROUPS * EXPERTS_PER_GROUP
TOP_K = 2
LN_EPS = 1e-5
MASK_VALUE = -1e30

COL_Q, COL_K, COL_V, COL_CB, COL_CC, COL_CX, COL_GA, COL_GC = range(8)

LANES = 128
BF16_SUBLANES = 16
ROUTER_LANES = LANES

LN_ROWS = 512
PROJ_TM = 1024
PROJ_TN = 1024
ATTN_T = 512
EXPERT_ROWS = 512
VMEM_LIMIT = 56 * 1024 * 1024


def _cparams(n_axes):
    return pltpu.CompilerParams(dimension_semantics=("arbitrary",) * n_axes,
                                vmem_limit_bytes=VMEM_LIMIT)


def _layer_norm(x, g, b):
    mu = jnp.mean(x, axis=-1, keepdims=True)
    xc = x - mu
    var = jnp.mean(xc * xc, axis=-1, keepdims=True)
    return xc * lax.rsqrt(var + LN_EPS) * g + b


def _ln_kernel(x_ref, g_ref, b_ref, h_ref, hb_ref):
    h = _layer_norm(x_ref[...], g_ref[...], b_ref[...])
    h_ref[...] = h
    hb_ref[...] = h.astype(BF16)


def _input_layer_norm(x, g, b):
    n, d = x.shape
    row = pl.BlockSpec((LN_ROWS, d), lambda i: (i, 0))
    vec = pl.BlockSpec((1, d), lambda i: (0, 0))
    return pl.pallas_call(
        _ln_kernel,
        grid=(n // LN_ROWS,),
        in_specs=[row, vec, vec],
        out_specs=[row, row],
        out_shape=[jax.ShapeDtypeStruct((n, d), F32), jax.ShapeDtypeStruct((n, d), BF16)],
        compiler_params=_cparams(1),
        name="input_layer_norm",
    )(x, g.reshape(1, d), b.reshape(1, d))


def _matmul_kernel(x_ref, w_ref, o_ref):
    o_ref[...] = jnp.dot(x_ref[...], w_ref[...], preferred_element_type=F32).astype(o_ref.dtype)


def _in_projection(hb, w):
    n, d = hb.shape
    c = w.shape[1]
    return pl.pallas_call(
        _matmul_kernel,
        grid=(n // PROJ_TM, c // PROJ_TN),
        in_specs=[pl.BlockSpec((PROJ_TM, d), lambda i, j: (i, 0)),
                  pl.BlockSpec((d, PROJ_TN), lambda i, j: (0, j))],
        out_specs=pl.BlockSpec((PROJ_TM, PROJ_TN), lambda i, j: (i, j)),
        out_shape=jax.ShapeDtypeStruct((n, c), BF16),
        compiler_params=_cparams(2),
        name="in_projection",
    )(hb, w)


def _rel_bucket(rel):
    n = jnp.maximum(rel, 0)
    max_exact = N_BUCKETS // 2
    nf = jnp.maximum(n, 1).astype(F32)
    large = max_exact + (jnp.log(nf / max_exact) / math.log(MAX_DISTANCE / max_exact)
                         * (N_BUCKETS - max_exact)).astype(jnp.int32)
    large = jnp.minimum(large, N_BUCKETS - 1)
    return jnp.where(n < max_exact, n, large)


def _bias_tiles(rel_table, t):
    assert t >= MAX_DISTANCE
    rel = jnp.arange(t, dtype=jnp.int32)[:, None] - jnp.arange(t, dtype=jnp.int32)[None, :]
    table = rel_table.astype(F32)
    diag = jnp.where((rel >= 0)[:, :, None], table[_rel_bucket(rel)], MASK_VALUE)
    prev = table[_rel_bucket(rel + t)]
    tiles = jnp.stack([diag, prev], axis=0)
    return jnp.transpose(tiles, (3, 0, 1, 2)), table[N_BUCKETS - 1]


def _attn_kernel(lam_ref, far_ref, q_ref, k_ref, v_ref, bias_ref, g_ref, o_ref,
                 m_ref, l_ref, acc_ref, *, t, post_scale):
    head = pl.program_id(0)
    qi = pl.program_id(2)
    q = q_ref[...] * (HEAD_DIM ** -0.5)
    lane = lax.broadcasted_iota(jnp.int32, q.shape, 1)
    zero = jnp.zeros_like(q)
    q_halves = (jnp.where(lane < HEAD_DIM, q, zero), jnp.where(lane >= HEAD_DIM, q, zero))

    m_ref[...] = jnp.full(m_ref.shape, MASK_VALUE, F32)
    l_ref[...] = jnp.zeros(l_ref.shape, F32)
    acc_ref[...] = jnp.zeros(acc_ref.shape, F32)

    def step(j, bias):
        rows = pl.ds(pl.multiple_of(j * t, t), t)
        k = k_ref[rows, :]
        v = v_ref[rows, :]
        for br in range(2):
            s = lax.dot_general(q_halves[br], k, (((1,), (1,)), ((), ())),
                                preferred_element_type=F32) + bias
            m_prev = m_ref[br]
            m_new = jnp.maximum(m_prev, jnp.max(s, axis=-1, keepdims=True))
            alpha = jnp.exp(m_prev - m_new)
            p = jnp.exp(s - m_new)
            l_ref[br] = alpha * l_ref[br] + jnp.sum(p, axis=-1, keepdims=True)
            acc_ref[br] = alpha * acc_ref[br] + jnp.dot(p.astype(BF16), v,
                                                        preferred_element_type=F32)
            m_ref[br] = m_new

    far_bias = far_ref[head]

    def far_step(j, carry):
        step(j, far_bias)
        return carry

    lax.fori_loop(0, jnp.maximum(qi - 1, 0), far_step, 0)

    @pl.when(qi >= 1)
    def _():
        step(qi - 1, bias_ref[1])

    step(qi, bias_ref[0])

    o = acc_ref[0] / l_ref[0] - lam_ref[0] * (acc_ref[1] / l_ref[1])
    ms = jnp.mean(o * o, axis=-1, keepdims=True)
    o_ref[...] = (o * lax.rsqrt(ms + LN_EPS) * g_ref[...] * post_scale).astype(o_ref.dtype)


def _diff_attention(z, lam, bias, far_bias, subln_g, batch, post_scale):
    n = z.shape[0]
    seq = n // batch
    t = ATTN_T
    nq = seq // t
    d_attn = N_HEADS * V_DIM
    smem = pl.BlockSpec(memory_space=pltpu.SMEM)
    return pl.pallas_call(
        functools.partial(_attn_kernel, t=t, post_scale=post_scale),
        grid=(N_HEADS, batch, nq),
        in_specs=[
            smem, smem,
            pl.BlockSpec((t, V_DIM), lambda h, b, i: (b * nq + i, COL_Q * N_HEADS + h)),
            pl.BlockSpec((seq, V_DIM), lambda h, b, i: (b, COL_K * N_HEADS + h)),
            pl.BlockSpec((seq, V_DIM), lambda h, b, i: (b, COL_V * N_HEADS + h)),
            pl.BlockSpec((None, 2, t, t), lambda h, b, i: (h, 0, 0, 0)),
            pl.BlockSpec((1, V_DIM), lambda h, b, i: (0, 0)),
        ],
        out_specs=pl.BlockSpec((t, V_DIM), lambda h, b, i: (b * nq + i, h)),
        out_shape=jax.ShapeDtypeStruct((n, d_attn), BF16),
        scratch_shapes=[pltpu.VMEM((2, t, 1), F32), pltpu.VMEM((2, t, 1), F32),
                        pltpu.VMEM((2, t, V_DIM), F32)],
        compiler_params=_cparams(3),
        name="diff_attention",
    )(lam.reshape(1), far_bias, z, z, z, bias, subln_g.reshape(1, V_DIM))


def _mixer_tail_kernel(on_ref, cb_ref, cc_ref, cx_ref, ga_ref, gc_ref, cch_ref, cxh_ref, h_ref,
                       wa_ref, wc_ref, wo_ref, convw_ref, gb_ref, lg_ref, lb_ref, wr_ref, br_ref,
                       h1_ref, h1b_ref, logit_ref, *, seq, alpha):
    i = pl.program_id(0)
    tm = on_ref.shape[0]
    attn = jnp.dot(on_ref[...], wa_ref[...], preferred_element_type=F32)

    u = cc_ref[...].astype(F32) * cx_ref[...].astype(F32)
    halo = cch_ref[...].astype(F32) * cxh_ref[...].astype(F32)
    halo = jnp.where((i * tm) % seq == 0, 0.0, halo)
    hm1 = halo[BF16_SUBLANES - 1:BF16_SUBLANES]
    hm2 = halo[BF16_SUBLANES - 2:BF16_SUBLANES - 1]
    row = lax.broadcasted_iota(jnp.int32, u.shape, 0)
    u1 = jnp.where(row == 0, hm1, pltpu.roll(u, 1, axis=0))
    u2 = jnp.where(row == 0, hm2, jnp.where(row == 1, hm1, pltpu.roll(u, 2, axis=0)))
    w = convw_ref[...]
    y = w[2:3] * u + w[1:2] * u1 + w[0:1] * u2
    conv = jnp.dot((cb_ref[...].astype(F32) * y).astype(BF16), wc_ref[...],
                   preferred_element_type=F32)

    gb = gb_ref[...]
    merged = (jax.nn.sigmoid(ga_ref[...].astype(F32) + gb[0:1]) * attn
              + jax.nn.sigmoid(gc_ref[...].astype(F32) + gb[1:2]) * conv)
    mix = jnp.dot(merged.astype(BF16), wo_ref[...], preferred_element_type=F32)
    h1 = _layer_norm(alpha * h_ref[...] + mix, lg_ref[...], lb_ref[...])
    h1b = h1.astype(BF16)
    h1_ref[...] = h1
    h1b_ref[...] = h1b
    logit_ref[...] = jnp.dot(h1b, wr_ref[...], preferred_element_type=F32) + br_ref[...]


def _mixer_tail(on, z, h, wa, wc, wo, conv_w, gate_b, ln_g, ln_b, wr, br, seq, alpha):
    n, d = h.shape
    tm = LN_ROWS
    halo_blocks = tm // BF16_SUBLANES

    def col(c):
        return pl.BlockSpec((tm, d), lambda i: (i, c))

    def halo(c):
        return pl.BlockSpec((BF16_SUBLANES, d), lambda i: (jnp.maximum(i * halo_blocks - 1, 0), c))

    def full(shape):
        return pl.BlockSpec(shape, lambda i: (0,) * len(shape))

    row = pl.BlockSpec((tm, d), lambda i: (i, 0))
    return pl.pallas_call(
        functools.partial(_mixer_tail_kernel, seq=seq, alpha=alpha),
        grid=(n // tm,),
        in_specs=[row, col(COL_CB), col(COL_CC), col(COL_CX), col(COL_GA), col(COL_GC),
                  halo(COL_CC), halo(COL_CX), row,
                  full((d, d)), full((d, d)), full((d, d)), full((CONV_K, d)), full((2, d)),
                  full((1, d)), full((1, d)), full((d, ROUTER_LANES)), full((1, ROUTER_LANES))],
        out_specs=[row, row, pl.BlockSpec((tm, ROUTER_LANES), lambda i: (i, 0))],
        out_shape=[jax.ShapeDtypeStruct((n, d), F32), jax.ShapeDtypeStruct((n, d), BF16),
                   jax.ShapeDtypeStruct((n, ROUTER_LANES), F32)],
        compiler_params=_cparams(1),
        name="mixer_tail",
    )(on, z, z, z, z, z, z, z, h, wa, wc, wo, conv_w, gate_b,
      ln_g.reshape(1, d), ln_b.reshape(1, d), wr, br)


def _route(logits, n_blk):
    n_tok = logits.shape[0]
    n_asg = n_tok * TOP_K
    g_prob = jax.nn.softmax(logits[:, :N_GROUPS], axis=-1)
    g_p, g_idx = lax.top_k(g_prob, 1)
    e_logits = logits[:, N_GROUPS:N_GROUPS + N_EXPERTS].reshape(n_tok, N_GROUPS, EXPERTS_PER_GROUP)
    e_logits = jnp.take_along_axis(e_logits, g_idx[:, :, None], axis=1)[:, 0]
    e_p, e_loc = lax.top_k(jax.nn.softmax(e_logits, axis=-1), TOP_K)
    gate = g_p * e_p / jnp.sum(e_p, axis=-1, keepdims=True)
    expert = (g_idx * EXPERTS_PER_GROUP + e_loc).reshape(-1).astype(jnp.int32)

    onehot = (expert[:, None] == jnp.arange(N_EXPERTS, dtype=jnp.int32)[None, :]).astype(jnp.int32)
    csum = jnp.cumsum(onehot, axis=0)
    rank = jnp.take_along_axis(csum, expert[:, None], axis=1)[:, 0] - 1
    sizes = csum[-1]
    padded = (sizes + EXPERT_ROWS - 1) // EXPERT_ROWS * EXPERT_ROWS
    pad_end = jnp.cumsum(padded)
    pad_start = pad_end - padded
    dest = pad_start[expert] + rank
    src_tok = jnp.zeros((n_blk * EXPERT_ROWS,), jnp.int32).at[dest].set(
        jnp.arange(n_asg, dtype=jnp.int32) // TOP_K)
    blk_e = jnp.minimum(jnp.searchsorted(pad_end, jnp.arange(n_blk, dtype=jnp.int32) * EXPERT_ROWS,
                                         side='right'), N_EXPERTS - 1).astype(jnp.int32)
    n_used = (pad_end[-1] // EXPERT_ROWS).astype(jnp.int32).reshape(1)
    return gate, dest.reshape(n_tok, TOP_K), src_tok, blk_e, n_used


def _expert_kernel(blk_e_ref, n_used_ref, x_ref, w1_ref, w3_ref, w2_ref, y_ref):
    i = pl.program_id(0)

    @pl.when(i < n_used_ref[0])
    def _():
        x = x_ref[...]
        a = jnp.dot(x, w1_ref[...], preferred_element_type=F32)
        b = jnp.dot(x, w3_ref[...], preferred_element_type=F32)
        mid = (a * jax.nn.sigmoid(a) * b).astype(BF16)
        y_ref[...] = jnp.dot(mid, w2_ref[...], preferred_element_type=F32).astype(y_ref.dtype)

    @pl.when(i >= n_used_ref[0])
    def _():
        y_ref[...] = jnp.zeros(y_ref.shape, y_ref.dtype)


def _expert_mlp(xbuf, blk_e, n_used, w1, w3, w2):
    n_rows, d = xbuf.shape
    de = w1.shape[-1]
    n_blk = n_rows // EXPERT_ROWS
    grid_spec = pltpu.PrefetchScalarGridSpec(
        num_scalar_prefetch=2,
        grid=(n_blk,),
        in_specs=[
            pl.BlockSpec((EXPERT_ROWS, d), lambda i, be, nu: (jnp.minimum(i, nu[0] - 1), 0)),
            pl.BlockSpec((None, d, de), lambda i, be, nu: (be[i], 0, 0)),
            pl.BlockSpec((None, d, de), lambda i, be, nu: (be[i], 0, 0)),
            pl.BlockSpec((None, de, d), lambda i, be, nu: (be[i], 0, 0)),
        ],
        out_specs=pl.BlockSpec((EXPERT_ROWS, d), lambda i, be, nu: (i, 0)),
    )
    return pl.pallas_call(
        _expert_kernel,
        grid_spec=grid_spec,
        out_shape=jax.ShapeDtypeStruct((n_rows, d), BF16),
        compiler_params=_cparams(1),
        name="expert_mlp",
    )(blk_e, n_used, xbuf, w1, w3, w2)


def _combine_kernel(h_ref, y0_ref, y1_ref, gate_ref, g_ref, b_ref, o_ref, ob_ref, *, alpha):
    gate = gate_ref[...]
    ffn = gate[:, 0:1] * y0_ref[...].astype(F32) + gate[:, 1:2] * y1_ref[...].astype(F32)
    h2 = _layer_norm(alpha * h_ref[...] + ffn, g_ref[...], b_ref[...])
    o_ref[...] = h2
    ob_ref[...] = h2.astype(BF16)


def _combine(h, y0, y1, gate, ln_g, ln_b, alpha):
    n, d = h.shape
    row = pl.BlockSpec((LN_ROWS, d), lambda i: (i, 0))
    vec = pl.BlockSpec((1, d), lambda i: (0, 0))
    return pl.pallas_call(
        functools.partial(_combine_kernel, alpha=alpha),
        grid=(n // LN_ROWS,),
        in_specs=[row, row, row, pl.BlockSpec((LN_ROWS, TOP_K), lambda i: (i, 0)), vec, vec],
        out_specs=[row, row],
        out_shape=[jax.ShapeDtypeStruct((n, d), F32), jax.ShapeDtypeStruct((n, d), BF16)],
        compiler_params=_cparams(1),
        name="moe_combine",
    )(h, y0, y1, gate, ln_g.reshape(1, d), ln_b.reshape(1, d))


def kernel(x, ln0_g, ln0_b, rel_table, w_in, gate_b, lam_vecs, subln_g, w_attn_proj, conv_w,
           w_conv_proj, w_out, ln1_g, ln1_b, w_rg, b_rg, w_re, b_re, w1, w3, w2, ln2_g, ln2_b):
    batch, seq, d = x.shape
    depth = w_in.shape[0]
    n_tok = batch * seq
    alpha = (2 * depth) ** 0.25
    assert seq % ATTN_T == 0 and n_tok % PROJ_TM == 0 and seq % LN_ROWS == 0
    n_asg = n_tok * TOP_K
    n_blk = -(-(n_asg + N_EXPERTS * (EXPERT_ROWS - 1)) // EXPERT_ROWS)

    bias, far_bias = _bias_tiles(rel_table, ATTN_T)
    h, hb = _input_layer_norm(x.reshape(n_tok, d), ln0_g, ln0_b)
    for l in range(depth):
        lam_init = 0.8 - 0.6 * math.exp(-0.3 * l)
        lv = lam_vecs[l].astype(F32)
        lam = jnp.exp(jnp.sum(lv[0] * lv[1])) - jnp.exp(jnp.sum(lv[2] * lv[3])) + lam_init

        z = _in_projection(hb, w_in[l].astype(BF16))
        on = _diff_attention(z, lam, bias, far_bias, subln_g[l], batch, 1.0 - lam_init)
        w_router = jnp.zeros((d, ROUTER_LANES), F32)
        w_router = w_router.at[:, :N_GROUPS].set(w_rg[l]).at[:, N_GROUPS:N_GROUPS + N_EXPERTS].set(w_re[l])
        b_router = jnp.zeros((1, ROUTER_LANES), F32)
        b_router = b_router.at[0, :N_GROUPS].set(b_rg[l]).at[0, N_GROUPS:N_GROUPS + N_EXPERTS].set(b_re[l])
        h1, h1b, logits = _mixer_tail(
            on, z, h, w_attn_proj[l].astype(BF16), w_conv_proj[l].astype(BF16), w_out[l].astype(BF16),
            conv_w[l], gate_b[l], ln1_g[l], ln1_b[l], w_router.astype(BF16), b_router, seq, alpha)

        gate, dest, src_tok, blk_e, n_used = _route(logits, n_blk)
        xbuf = h1b[src_tok]
        ybuf = _expert_mlp(xbuf, blk_e, n_used, w1[l].astype(BF16), w3[l].astype(BF16),
                           w2[l].astype(BF16))
        h, hb = _combine(h1, ybuf[dest[:, 0]], ybuf[dest[:, 1]], gate, ln2_g[l], ln2_b[l], alpha)
    return h.reshape(batch, seq, d)
```

```python
import functools
import math

import jax
import jax.numpy as jnp
from jax import lax
from jax.experimental import pallas as pl
from jax.experimental.pallas import tpu as pltpu

F32 = jnp.float32
BF16 = jnp.bfloat16

N_HEADS = 8
HEAD_DIM = 64
V_DIM = 2 * HEAD_DIM
N_BUCKETS = 32
MAX_DISTANCE = 128
CONV_K = 3
N_GROUPS = 4
EXPERTS_PER_GROUP = 8
N_EXPERTS = N_GROUPS * EXPERTS_PER_GROUP
TOP_K = 2
LN_EPS = 1e-5
MASK_VALUE = -1e30

IN_Q, IN_K, IN_V, IN_CB, IN_CC, IN_CX, IN_GA, IN_GC = range(8)
COL_Q, COL_K, COL_CB, COL_CC, COL_CX, COL_GA, COL_GC = range(7)

LANES = 128
BF16_SUBLANES = 16
ROUTER_LANES = LANES

LN_ROWS = 512
PROJ_TM = 1024
PROJ_TN = 1024
ATTN_T = 512
EXPERT_ROWS = 512
VMEM_LIMIT = 56 * 1024 * 1024


def _cparams(n_axes):
    return pltpu.CompilerParams(dimension_semantics=("arbitrary",) * n_axes,
                                vmem_limit_bytes=VMEM_LIMIT)


def _layer_norm(x, g, b):
    mu = jnp.mean(x, axis=-1, keepdims=True)
    xc = x - mu
    var = jnp.mean(xc * xc, axis=-1, keepdims=True)
    return xc * lax.rsqrt(var + LN_EPS) * g + b


def _ln_kernel(x_ref, g_ref, b_ref, h_ref, hb_ref):
    h = _layer_norm(x_ref[...], g_ref[...], b_ref[...])
    h_ref[...] = h
    hb_ref[...] = h.astype(BF16)


def _input_layer_norm(x, g, b):
    n, d = x.shape
    row = pl.BlockSpec((LN_ROWS, d), lambda i: (i, 0))
    vec = pl.BlockSpec((1, d), lambda i: (0, 0))
    return pl.pallas_call(
        _ln_kernel,
        grid=(n // LN_ROWS,),
        in_specs=[row, vec, vec],
        out_specs=[row, row],
        out_shape=[jax.ShapeDtypeStruct((n, d), F32), jax.ShapeDtypeStruct((n, d), BF16)],
        compiler_params=_cparams(1),
        name="input_layer_norm",
    )(x, g.reshape(1, d), b.reshape(1, d))


def _matmul_kernel(x_ref, w_ref, o_ref):
    o_ref[...] = jnp.dot(x_ref[...], w_ref[...], preferred_element_type=F32).astype(o_ref.dtype)


def _in_projection(hb, w):
    n, d = hb.shape
    c = w.shape[1]
    return pl.pallas_call(
        _matmul_kernel,
        grid=(n // PROJ_TM, c // PROJ_TN),
        in_specs=[pl.BlockSpec((PROJ_TM, d), lambda i, j: (i, 0)),
                  pl.BlockSpec((d, PROJ_TN), lambda i, j: (0, j))],
        out_specs=pl.BlockSpec((PROJ_TM, PROJ_TN), lambda i, j: (i, j)),
        out_shape=jax.ShapeDtypeStruct((n, c), BF16),
        compiler_params=_cparams(2),
        name="in_projection",
    )(hb, w)


def _matmul_nt_kernel(w_ref, x_ref, o_ref):
    o_ref[...] = lax.dot_general(w_ref[...], x_ref[...], (((1,), (1,)), ((), ())),
                                 preferred_element_type=F32).astype(o_ref.dtype)


def _v_projection_t(hb, wvt):
    n, d = hb.shape
    c = wvt.shape[0]
    t = ATTN_T
    return pl.pallas_call(
        _matmul_nt_kernel,
        grid=(n // t,),
        in_specs=[pl.BlockSpec((c, d), lambda i: (0, 0)),
                  pl.BlockSpec((t, d), lambda i: (i, 0))],
        out_specs=pl.BlockSpec((None, c, t), lambda i: (i, 0, 0)),
        out_shape=jax.ShapeDtypeStruct((n // t, c, t), BF16),
        compiler_params=_cparams(1),
        name="v_projection_t",
    )(wvt, hb)


def _rel_bucket(rel):
    n = jnp.maximum(rel, 0)
    max_exact = N_BUCKETS // 2
    nf = jnp.maximum(n, 1).astype(F32)
    large = max_exact + (jnp.log(nf / max_exact) / math.log(MAX_DISTANCE / max_exact)
                         * (N_BUCKETS - max_exact)).astype(jnp.int32)
    large = jnp.minimum(large, N_BUCKETS - 1)
    return jnp.where(n < max_exact, n, large)


def _bias_tiles(rel_table, t):
    assert t >= MAX_DISTANCE
    rel = jnp.arange(t, dtype=jnp.int32)[None, :] - jnp.arange(t, dtype=jnp.int32)[:, None]
    table = rel_table.astype(F32)
    diag = jnp.where((rel >= 0)[:, :, None], table[_rel_bucket(rel)], MASK_VALUE)
    prev = table[_rel_bucket(rel + t)]
    tiles = jnp.stack([diag, prev], axis=0)
    return jnp.transpose(tiles, (3, 0, 1, 2)), table[N_BUCKETS - 1]


def _attn_kernel(lam_ref, far_ref, q_ref, k_ref, vt_ref, bias_ref, g_ref, o_ref,
                 m_ref, acc_ref, *, t, post_scale):
    head = pl.program_id(0)
    qi = pl.program_id(2)
    q = q_ref[...] * (HEAD_DIM ** -0.5)
    lane = lax.broadcasted_iota(jnp.int32, q.shape, 1)
    zero = jnp.zeros_like(q)
    q_halves = (jnp.where(lane < HEAD_DIM, q, zero), jnp.where(lane >= HEAD_DIM, q, zero))
    ones = jnp.ones((BF16_SUBLANES, t), BF16)

    m_ref[...] = jnp.full(m_ref.shape, MASK_VALUE, F32)
    acc_ref[...] = jnp.zeros(acc_ref.shape, F32)

    def step(j, bias, shift):
        k = k_ref[pl.ds(pl.multiple_of(j * t, t), t), :]
        v_aug = jnp.concatenate([vt_ref[j], ones], axis=0)
        for br in range(2):
            s = lax.dot_general(k, q_halves[br], (((1,), (1,)), ((), ())),
                                preferred_element_type=F32)
            if bias is not None:
                s = s + bias
            m_prev = m_ref[br]
            m_tile = jnp.max(s, axis=0, keepdims=True)
            if shift is not None:
                m_tile = m_tile + shift
            m_new = jnp.maximum(m_prev, m_tile)
            alpha = jnp.exp(m_prev - m_new)
            p = jnp.exp(s - (m_new if shift is None else m_new - shift)).astype(BF16)
            acc_ref[br] = alpha * acc_ref[br] + jnp.dot(v_aug, p, preferred_element_type=F32)
            m_ref[br] = m_new

    far_bias = far_ref[head]

    def far_step(j, carry):
        step(j, None, far_bias)
        return carry

    lax.fori_loop(0, jnp.maximum(qi - 1, 0), far_step, 0)

    @pl.when(qi >= 1)
    def _():
        step(qi - 1, bias_ref[1], None)

    step(qi, bias_ref[0], None)

    a1 = acc_ref[0]
    a2 = acc_ref[1]
    o = a1[:V_DIM] / a1[V_DIM:V_DIM + 1] - lam_ref[0] * (a2[:V_DIM] / a2[V_DIM:V_DIM + 1])
    ms = jnp.mean(o * o, axis=0, keepdims=True)
    on = o * lax.rsqrt(ms + LN_EPS) * (g_ref[...] * post_scale)
    o_ref[...] = on.T.astype(o_ref.dtype)


def _diff_attention(z, vt, lam, bias, far_bias, subln_g, batch, post_scale):
    n = z.shape[0]
    seq = n // batch
    t = ATTN_T
    nq = seq // t
    d_attn = N_HEADS * V_DIM
    smem = pl.BlockSpec(memory_space=pltpu.SMEM)
    return pl.pallas_call(
        functools.partial(_attn_kernel, t=t, post_scale=post_scale),
        grid=(N_HEADS, batch, nq),
        in_specs=[
            smem, smem,
            pl.BlockSpec((t, V_DIM), lambda h, b, i: (b * nq + i, COL_Q * N_HEADS + h)),
            pl.BlockSpec((seq, V_DIM), lambda h, b, i: (b, COL_K * N_HEADS + h)),
            pl.BlockSpec((nq, V_DIM, t), lambda h, b, i: (b, h, 0)),
            pl.BlockSpec((None, 2, t, t), lambda h, b, i: (h, 0, 0, 0)),
            pl.BlockSpec((V_DIM, 1), lambda h, b, i: (0, 0)),
        ],
        out_specs=pl.BlockSpec((t, V_DIM), lambda h, b, i: (b * nq + i, h)),
        out_shape=jax.ShapeDtypeStruct((n, d_attn), BF16),
        scratch_shapes=[pltpu.VMEM((2, 1, t), F32),
                        pltpu.VMEM((2, V_DIM + BF16_SUBLANES, t), F32)],
        compiler_params=_cparams(3),
        name="diff_attention",
    )(lam.reshape(1), far_bias, z, z, vt, bias, subln_g.reshape(V_DIM, 1))


def _mixer_tail_kernel(on_ref, cb_ref, cc_ref, cx_ref, ga_ref, gc_ref, cch_ref, cxh_ref, h_ref,
                       wa_ref, wc_ref, wo_ref, convw_ref, gb_ref, lg_ref, lb_ref, wr_ref, br_ref,
                       h1_ref, h1b_ref, logit_ref, *, seq, alpha):
    i = pl.program_id(0)
    tm = on_ref.shape[0]
    attn = jnp.dot(on_ref[...], wa_ref[...], preferred_element_type=F32)

    u = cc_ref[...].astype(F32) * cx_ref[...].astype(F32)
    halo = cch_ref[...].astype(F32) * cxh_ref[...].astype(F32)
    halo = jnp.where((i * tm) % seq == 0, 0.0, halo)
    hm1 = halo[BF16_SUBLANES - 1:BF16_SUBLANES]
    hm2 = halo[BF16_SUBLANES - 2:BF16_SUBLANES - 1]
    row = lax.broadcasted_iota(jnp.int32, u.shape, 0)
    u1 = jnp.where(row == 0, hm1, pltpu.roll(u, 1, axis=0))
    u2 = jnp.where(row == 0, hm2, jnp.where(row == 1, hm1, pltpu.roll(u, 2, axis=0)))
    w = convw_ref[...]
    y = w[2:3] * u + w[1:2] * u1 + w[0:1] * u2
    conv = jnp.dot((cb_ref[...].astype(F32) * y).astype(BF16), wc_ref[...],
                   preferred_element_type=F32)

    gb = gb_ref[...]
    merged = (jax.nn.sigmoid(ga_ref[...].astype(F32) + gb[0:1]) * attn
              + jax.nn.sigmoid(gc_ref[...].astype(F32) + gb[1:2]) * conv)
    mix = jnp.dot(merged.astype(BF16), wo_ref[...], preferred_element_type=F32)
    h1 = _layer_norm(alpha * h_ref[...] + mix, lg_ref[...], lb_ref[...])
    h1b = h1.astype(BF16)
    h1_ref[...] = h1
    h1b_ref[...] = h1b
    logit_ref[...] = jnp.dot(h1b, wr_ref[...], preferred_element_type=F32) + br_ref[...]


def _mixer_tail(on, z, h, wa, wc, wo, conv_w, gate_b, ln_g, ln_b, wr, br, seq, alpha):
    n, d = h.shape
    tm = LN_ROWS
    halo_blocks = tm // BF16_SUBLANES

    def col(c):
        return pl.BlockSpec((tm, d), lambda i: (i, c))

    def halo(c):
        return pl.BlockSpec((BF16_SUBLANES, d), lambda i: (jnp.maximum(i * halo_blocks - 1, 0), c))

    def full(shape):
        return pl.BlockSpec(shape, lambda i: (0,) * len(shape))

    row = pl.BlockSpec((tm, d), lambda i: (i, 0))
    return pl.pallas_call(
        functools.partial(_mixer_tail_kernel, seq=seq, alpha=alpha),
        grid=(n // tm,),
        in_specs=[row, col(COL_CB), col(COL_CC), col(COL_CX), col(COL_GA), col(COL_GC),
                  halo(COL_CC), halo(COL_CX), row,
                  full((d, d)), full((d, d)), full((d, d)), full((CONV_K, d)), full((2, d)),
                  full((1, d)), full((1, d)), full((d, ROUTER_LANES)), full((1, ROUTER_LANES))],
        out_specs=[row, row, pl.BlockSpec((tm, ROUTER_LANES), lambda i: (i, 0))],
        out_shape=[jax.ShapeDtypeStruct((n, d), F32), jax.ShapeDtypeStruct((n, d), BF16),
                   jax.ShapeDtypeStruct((n, ROUTER_LANES), F32)],
        compiler_params=_cparams(1),
        name="mixer_tail",
    )(on, z, z, z, z, z, z, z, h, wa, wc, wo, conv_w, gate_b,
      ln_g.reshape(1, d), ln_b.reshape(1, d), wr, br)


def _route(logits, n_blk):
    n_tok = logits.shape[0]
    n_asg = n_tok * TOP_K
    gl = logits[:, :N_GROUPS]
    g_max = jnp.max(gl, axis=-1, keepdims=True)
    g_idx = jnp.argmax(gl, axis=-1)[:, None].astype(jnp.int32)
    g_p = 1.0 / jnp.sum(jnp.exp(gl - g_max), axis=-1, keepdims=True)
    e_all = logits[:, N_GROUPS:N_GROUPS + N_EXPERTS].reshape(n_tok, N_GROUPS, EXPERTS_PER_GROUP)
    in_group = g_idx[:, :, None] == jnp.arange(N_GROUPS, dtype=jnp.int32)[None, :, None]
    e_logits = jnp.sum(jnp.where(in_group, e_all, 0.0), axis=1)
    loc = jnp.arange(EXPERTS_PER_GROUP, dtype=jnp.int32)[None, :]
    i0 = jnp.argmax(e_logits, axis=-1)[:, None].astype(jnp.int32)
    e0 = jnp.max(e_logits, axis=-1, keepdims=True)
    rest = jnp.where(loc == i0, -jnp.inf, e_logits)
    i1 = jnp.argmax(rest, axis=-1)[:, None].astype(jnp.int32)
    e1 = jnp.max(rest, axis=-1, keepdims=True)
    r = jnp.exp(e1 - e0)
    gate = jnp.concatenate([g_p / (1.0 + r), g_p * r / (1.0 + r)], axis=-1)
    e_loc = jnp.concatenate([i0, i1], axis=-1)
    expert = (g_idx * EXPERTS_PER_GROUP + e_loc).reshape(-1).astype(jnp.int32)

    onehot = (expert[:, None] == jnp.arange(N_EXPERTS, dtype=jnp.int32)[None, :]).astype(jnp.int32)
    csum = jnp.cumsum(onehot, axis=0)
    sizes = csum[-1]
    padded = (sizes + EXPERT_ROWS - 1) // EXPERT_ROWS * EXPERT_ROWS
    pad_end = jnp.cumsum(padded)
    pad_start = pad_end - padded
    dest = jnp.sum(onehot * (csum - 1 + pad_start[None, :]), axis=1)
    src_tok = jnp.zeros((n_blk * EXPERT_ROWS,), jnp.int32).at[dest].set(
        jnp.arange(n_asg, dtype=jnp.int32) // TOP_K)
    blk_e = jnp.minimum(jnp.searchsorted(pad_end, jnp.arange(n_blk, dtype=jnp.int32) * EXPERT_ROWS,
                                         side='right'), N_EXPERTS - 1).astype(jnp.int32)
    n_used = (pad_end[-1] // EXPERT_ROWS).astype(jnp.int32).reshape(1)
    return gate, dest.reshape(n_tok, TOP_K), src_tok, blk_e, n_used


def _expert_kernel(blk_e_ref, n_used_ref, x_ref, w1_ref, w3_ref, w2_ref, y_ref):
    i = pl.program_id(0)

    @pl.when(i < n_used_ref[0])
    def _():
        x = x_ref[...]
        a = jnp.dot(x, w1_ref[...], preferred_element_type=F32)
        b = jnp.dot(x, w3_ref[...], preferred_element_type=F32)
        mid = (a * jax.nn.sigmoid(a) * b).astype(BF16)
        y_ref[...] = jnp.dot(mid, w2_ref[...], preferred_element_type=F32).astype(y_ref.dtype)

    @pl.when(i >= n_used_ref[0])
    def _():
        y_ref[...] = jnp.zeros(y_ref.shape, y_ref.dtype)


def _expert_mlp(xbuf, blk_e, n_used, w1, w3, w2):
    n_rows, d = xbuf.shape
    de = w1.shape[-1]
    n_blk = n_rows // EXPERT_ROWS
    grid_spec = pltpu.PrefetchScalarGridSpec(
        num_scalar_prefetch=2,
        grid=(n_blk,),
        in_specs=[
            pl.BlockSpec((EXPERT_ROWS, d), lambda i, be, nu: (jnp.minimum(i, nu[0] - 1), 0)),
            pl.BlockSpec((None, d, de), lambda i, be, nu: (be[i], 0, 0)),
            pl.BlockSpec((None, d, de), lambda i, be, nu: (be[i], 0, 0)),
            pl.BlockSpec((None, de, d), lambda i, be, nu: (be[i], 0, 0)),
        ],
        out_specs=pl.BlockSpec((EXPERT_ROWS, d), lambda i, be, nu: (i, 0)),
    )
    return pl.pallas_call(
        _expert_kernel,
        grid_spec=grid_spec,
        out_shape=jax.ShapeDtypeStruct((n_rows, d), BF16),
        compiler_params=_cparams(1),
        name="expert_mlp",
    )(blk_e, n_used, xbuf, w1, w3, w2)


def _combine_kernel(h_ref, y0_ref, y1_ref, gate_ref, g_ref, b_ref, o_ref, ob_ref, *, alpha):
    gate = gate_ref[...]
    ffn = gate[:, 0:1] * y0_ref[...].astype(F32) + gate[:, 1:2] * y1_ref[...].astype(F32)
    h2 = _layer_norm(alpha * h_ref[...] + ffn, g_ref[...], b_ref[...])
    o_ref[...] = h2
    ob_ref[...] = h2.astype(BF16)


def _combine(h, y0, y1, gate, ln_g, ln_b, alpha):
    n, d = h.shape
    row = pl.BlockSpec((LN_ROWS, d), lambda i: (i, 0))
    vec = pl.BlockSpec((1, d), lambda i: (0, 0))
    return pl.pallas_call(
        functools.partial(_combine_kernel, alpha=alpha),
        grid=(n // LN_ROWS,),
        in_specs=[row, row, row, pl.BlockSpec((LN_ROWS, TOP_K), lambda i: (i, 0)), vec, vec],
        out_specs=[row, row],
        out_shape=[jax.ShapeDtypeStruct((n, d), F32), jax.ShapeDtypeStruct((n, d), BF16)],
        compiler_params=_cparams(1),
        name="moe_combine",
    )(h, y0, y1, gate, ln_g.reshape(1, d), ln_b.reshape(1, d))


def kernel(x, ln0_g, ln0_b, rel_table, w_in, gate_b, lam_vecs, subln_g, w_attn_proj, conv_w,
           w_conv_proj, w_out, ln1_g, ln1_b, w_rg, b_rg, w_re, b_re, w1, w3, w2, ln2_g, ln2_b):
    batch, seq, d = x.shape
    depth = w_in.shape[0]
    n_tok = batch * seq
    alpha = (2 * depth) ** 0.25
    assert seq % ATTN_T == 0 and n_tok % PROJ_TM == 0 and seq % LN_ROWS == 0
    n_asg = n_tok * TOP_K
    n_blk = -(-(n_asg + N_EXPERTS * (EXPERT_ROWS - 1)) // EXPERT_ROWS)

    bias, far_bias = _bias_tiles(rel_table, ATTN_T)
    h, hb = _input_layer_norm(x.reshape(n_tok, d), ln0_g, ln0_b)
    for l in range(depth):
        lam_init = 0.8 - 0.6 * math.exp(-0.3 * l)
        lv = lam_vecs[l].astype(F32)
        lam = jnp.exp(jnp.sum(lv[0] * lv[1])) - jnp.exp(jnp.sum(lv[2] * lv[3])) + lam_init

        w_l = w_in[l]
        w_tok = jnp.concatenate([w_l[:, :IN_V * d], w_l[:, IN_CB * d:]], axis=1).astype(BF16)
        w_vt = w_l[:, IN_V * d:IN_CB * d].T.astype(BF16)
        z = _in_projection(hb, w_tok)
        vt = _v_projection_t(hb, w_vt)
        on = _diff_attention(z, vt, lam, bias, far_bias, subln_g[l], batch, 1.0 - lam_init)
        w_router = jnp.zeros((d, ROUTER_LANES), F32)
        w_router = w_router.at[:, :N_GROUPS].set(w_rg[l]).at[:, N_GROUPS:N_GROUPS + N_EXPERTS].set(w_re[l])
        b_router = jnp.zeros((1, ROUTER_LANES), F32)
        b_router = b_router.at[0, :N_GROUPS].set(b_rg[l]).at[0, N_GROUPS:N_GROUPS + N_EXPERTS].set(b_re[l])
        h1, h1b, logits = _mixer_tail(
            on, z, h, w_attn_proj[l].astype(BF16), w_conv_proj[l].astype(BF16), w_out[l].astype(BF16),
            conv_w[l], gate_b[l], ln1_g[l], ln1_b[l], w_router.astype(BF16), b_router, seq, alpha)

        gate, dest, src_tok, blk_e, n_used = _route(logits, n_blk)
        xbuf = h1b[src_tok]
        ybuf = _expert_mlp(xbuf, blk_e, n_used, w1[l].astype(BF16), w3[l].astype(BF16),
                           w2[l].astype(BF16))
        h, hb = _combine(h1, ybuf[dest[:, 0]], ybuf[dest[:, 1]], gate, ln2_g[l], ln2_b[l], alpha)
    return h.reshape(batch, seq, d)
```

```python
import functools
import math

import jax
import jax.numpy as jnp
from jax import lax
from jax.experimental import pallas as pl
from jax.experimental.pallas import tpu as pltpu

F32 = jnp.float32
BF16 = jnp.bfloat16

N_HEADS = 8
HEAD_DIM = 64
V_DIM = 2 * HEAD_DIM
N_BUCKETS = 32
MAX_DISTANCE = 128
CONV_K = 3
N_GROUPS = 4
EXPERTS_PER_GROUP = 8
N_EXPERTS = N_GROUPS * EXPERTS_PER_GROUP
TOP_K = 2
LN_EPS = 1e-5
MASK_VALUE = -1e30

IN_Q, IN_K, IN_V, IN_CB, IN_CC, IN_CX, IN_GA, IN_GC = range(8)
COL_Q, COL_K, COL_CB, COL_CC, COL_CX, COL_GA, COL_GC = range(7)

LANES = 128
BF16_SUBLANES = 16
ROUTER_LANES = LANES

LN_ROWS = 512
PROJ_TM = 1024
PROJ_TN = 1024
ATTN_T = 512
EXPERT_ROWS = 512
VMEM_LIMIT = 56 * 1024 * 1024


def _cparams(n_axes):
    return pltpu.CompilerParams(dimension_semantics=("arbitrary",) * n_axes,
                                vmem_limit_bytes=VMEM_LIMIT)


def _layer_norm(x, g, b):
    mu = jnp.mean(x, axis=-1, keepdims=True)
    xc = x - mu
    var = jnp.mean(xc * xc, axis=-1, keepdims=True)
    return xc * lax.rsqrt(var + LN_EPS) * g + b


def _ln_kernel(x_ref, g_ref, b_ref, h_ref, hb_ref):
    h = _layer_norm(x_ref[...], g_ref[...], b_ref[...])
    h_ref[...] = h
    hb_ref[...] = h.astype(BF16)


def _input_layer_norm(x, g, b):
    n, d = x.shape
    row = pl.BlockSpec((LN_ROWS, d), lambda i: (i, 0))
    vec = pl.BlockSpec((1, d), lambda i: (0, 0))
    return pl.pallas_call(
        _ln_kernel,
        grid=(n // LN_ROWS,),
        in_specs=[row, vec, vec],
        out_specs=[row, row],
        out_shape=[jax.ShapeDtypeStruct((n, d), F32), jax.ShapeDtypeStruct((n, d), BF16)],
        compiler_params=_cparams(1),
        name="input_layer_norm",
    )(x, g.reshape(1, d), b.reshape(1, d))


def _matmul_kernel(x_ref, w_ref, o_ref):
    o_ref[...] = jnp.dot(x_ref[...], w_ref[...], preferred_element_type=F32).astype(o_ref.dtype)


def _in_projection(hb, w):
    n, d = hb.shape
    c = w.shape[1]
    return pl.pallas_call(
        _matmul_kernel,
        grid=(n // PROJ_TM, c // PROJ_TN),
        in_specs=[pl.BlockSpec((PROJ_TM, d), lambda i, j: (i, 0)),
                  pl.BlockSpec((d, PROJ_TN), lambda i, j: (0, j))],
        out_specs=pl.BlockSpec((PROJ_TM, PROJ_TN), lambda i, j: (i, j)),
        out_shape=jax.ShapeDtypeStruct((n, c), BF16),
        compiler_params=_cparams(2),
        name="in_projection",
    )(hb, w)


def _matmul_nt_kernel(w_ref, x_ref, o_ref):
    o_ref[...] = lax.dot_general(w_ref[...], x_ref[...], (((1,), (1,)), ((), ())),
                                 preferred_element_type=F32).astype(o_ref.dtype)


def _v_projection_t(hb, wvt):
    n, d = hb.shape
    c = wvt.shape[0]
    t = ATTN_T
    return pl.pallas_call(
        _matmul_nt_kernel,
        grid=(n // t,),
        in_specs=[pl.BlockSpec((c, d), lambda i: (0, 0)),
                  pl.BlockSpec((t, d), lambda i: (i, 0))],
        out_specs=pl.BlockSpec((None, c, t), lambda i: (i, 0, 0)),
        out_shape=jax.ShapeDtypeStruct((n // t, c, t), BF16),
        compiler_params=_cparams(1),
        name="v_projection_t",
    )(wvt, hb)


def _rel_bucket(rel):
    n = jnp.maximum(rel, 0)
    max_exact = N_BUCKETS // 2
    nf = jnp.maximum(n, 1).astype(F32)
    large = max_exact + (jnp.log(nf / max_exact) / math.log(MAX_DISTANCE / max_exact)
                         * (N_BUCKETS - max_exact)).astype(jnp.int32)
    large = jnp.minimum(large, N_BUCKETS - 1)
    return jnp.where(n < max_exact, n, large)


def _bias_tiles(rel_table, t):
    assert t >= MAX_DISTANCE
    rel = jnp.arange(t, dtype=jnp.int32)[None, :] - jnp.arange(t, dtype=jnp.int32)[:, None]
    table = rel_table.astype(F32)
    diag = jnp.where((rel >= 0)[:, :, None], table[_rel_bucket(rel)], MASK_VALUE)
    prev = table[_rel_bucket(rel + t)]
    tiles = jnp.stack([diag, prev], axis=0)
    return jnp.transpose(tiles, (3, 0, 1, 2)), table[N_BUCKETS - 1]


def _attn_kernel(lam_ref, far_ref, q_ref, k_ref, vt_ref, bias_ref, g_ref, o_ref,
                 m_ref, acc_ref, sa_ref, sb_ref, *, t, post_scale):
    head = pl.program_id(0)
    qi = pl.program_id(2)
    q = q_ref[...] * (HEAD_DIM ** -0.5)
    lane = lax.broadcasted_iota(jnp.int32, q.shape, 1)
    zero = jnp.zeros_like(q)
    q_halves = (jnp.where(lane < HEAD_DIM, q, zero), jnp.where(lane >= HEAD_DIM, q, zero))
    ones = jnp.ones((BF16_SUBLANES, t), BF16)

    m_ref[...] = jnp.full(m_ref.shape, MASK_VALUE, F32)
    acc_ref[...] = jnp.zeros(acc_ref.shape, F32)

    def scores(j, s_ref):
        k = k_ref[pl.ds(pl.multiple_of(j * t, t), t), :]
        for br in range(2):
            s_ref[br] = lax.dot_general(k, q_halves[br], (((1,), (1,)), ((), ())),
                                        preferred_element_type=F32)

    def softmax_pv(j, s_ref, bias, shift):
        v_aug = jnp.concatenate([vt_ref[j], ones], axis=0)
        for br in range(2):
            s = s_ref[br]
            if bias is not None:
                s = s + bias
            m_prev = m_ref[br]
            m_tile = jnp.max(s, axis=0, keepdims=True)
            if shift is not None:
                m_tile = m_tile + shift
            m_new = jnp.maximum(m_prev, m_tile)
            alpha = jnp.exp(m_prev - m_new)
            p = jnp.exp(s - (m_new if shift is None else m_new - shift)).astype(BF16)
            acc_ref[br] = alpha * acc_ref[br] + jnp.dot(v_aug, p, preferred_element_type=F32)
            m_ref[br] = m_new

    far_bias = far_ref[head]

    @pl.when(qi >= 1)
    def _():
        n_far = qi - 1
        odd = n_far % 2

        @pl.when(odd == 1)
        def _():
            scores(0, sb_ref)
            softmax_pv(0, sb_ref, None, far_bias)

        scores(odd, sa_ref)

        def pair(i, carry):
            j = odd + 2 * i
            scores(j + 1, sb_ref)
            softmax_pv(j, sa_ref, None, far_bias)
            scores(j + 2, sa_ref)
            softmax_pv(j + 1, sb_ref, None, far_bias)
            return carry

        lax.fori_loop(0, n_far // 2, pair, 0)

    scores(qi, sb_ref)

    @pl.when(qi >= 1)
    def _():
        softmax_pv(qi - 1, sa_ref, bias_ref[1], None)

    softmax_pv(qi, sb_ref, bias_ref[0], None)

    a1 = acc_ref[0]
    a2 = acc_ref[1]
    o = a1[:V_DIM] / a1[V_DIM:V_DIM + 1] - lam_ref[0] * (a2[:V_DIM] / a2[V_DIM:V_DIM + 1])
    ms = jnp.mean(o * o, axis=0, keepdims=True)
    on = o * lax.rsqrt(ms + LN_EPS) * (g_ref[...] * post_scale)
    o_ref[...] = on.T.astype(o_ref.dtype)


def _diff_attention(z, vt, lam, bias, far_bias, subln_g, batch, post_scale):
    n = z.shape[0]
    seq = n // batch
    t = ATTN_T
    nq = seq // t
    d_attn = N_HEADS * V_DIM
    smem = pl.BlockSpec(memory_space=pltpu.SMEM)
    return pl.pallas_call(
        functools.partial(_attn_kernel, t=t, post_scale=post_scale),
        grid=(N_HEADS, batch, nq),
        in_specs=[
            smem, smem,
            pl.BlockSpec((t, V_DIM), lambda h, b, i: (b * nq + i, COL_Q * N_HEADS + h)),
            pl.BlockSpec((seq, V_DIM), lambda h, b, i: (b, COL_K * N_HEADS + h)),
            pl.BlockSpec((nq, V_DIM, t), lambda h, b, i: (b, h, 0)),
            pl.BlockSpec((None, 2, t, t), lambda h, b, i: (h, 0, 0, 0)),
            pl.BlockSpec((V_DIM, 1), lambda h, b, i: (0, 0)),
        ],
        out_specs=pl.BlockSpec((t, V_DIM), lambda h, b, i: (b * nq + i, h)),
        out_shape=jax.ShapeDtypeStruct((n, d_attn), BF16),
        scratch_shapes=[pltpu.VMEM((2, 1, t), F32),
                        pltpu.VMEM((2, V_DIM + BF16_SUBLANES, t), F32),
                        pltpu.VMEM((2, t, t), F32), pltpu.VMEM((2, t, t), F32)],
        compiler_params=_cparams(3),
        name="diff_attention",
    )(lam.reshape(1), far_bias, z, z, vt, bias, subln_g.reshape(V_DIM, 1))


def _mixer_tail_kernel(on_ref, cb_ref, cc_ref, cx_ref, ga_ref, gc_ref, cch_ref, cxh_ref, h_ref,
                       wa_ref, wc_ref, wo_ref, convw_ref, gb_ref, lg_ref, lb_ref, wr_ref, br_ref,
                       h1_ref, h1b_ref, logit_ref, *, seq, alpha):
    i = pl.program_id(0)
    tm = on_ref.shape[0]
    attn = jnp.dot(on_ref[...], wa_ref[...], preferred_element_type=F32)

    u = cc_ref[...].astype(F32) * cx_ref[...].astype(F32)
    halo = cch_ref[...].astype(F32) * cxh_ref[...].astype(F32)
    halo = jnp.where((i * tm) % seq == 0, 0.0, halo)
    hm1 = halo[BF16_SUBLANES - 1:BF16_SUBLANES]
    hm2 = halo[BF16_SUBLANES - 2:BF16_SUBLANES - 1]
    row = lax.broadcasted_iota(jnp.int32, u.shape, 0)
    u1 = jnp.where(row == 0, hm1, pltpu.roll(u, 1, axis=0))
    u2 = jnp.where(row == 0, hm2, jnp.where(row == 1, hm1, pltpu.roll(u, 2, axis=0)))
    w = convw_ref[...]
    y = w[2:3] * u + w[1:2] * u1 + w[0:1] * u2
    conv = jnp.dot((cb_ref[...].astype(F32) * y).astype(BF16), wc_ref[...],
                   preferred_element_type=F32)

    gb = gb_ref[...]
    merged = (jax.nn.sigmoid(ga_ref[...].astype(F32) + gb[0:1]) * attn
              + jax.nn.sigmoid(gc_ref[...].astype(F32) + gb[1:2]) * conv)
    mix = jnp.dot(merged.astype(BF16), wo_ref[...], preferred_element_type=F32)
    h1 = _layer_norm(alpha * h_ref[...] + mix, lg_ref[...], lb_ref[...])
    h1b = h1.astype(BF16)
    h1_ref[...] = h1
    h1b_ref[...] = h1b
    logit_ref[...] = jnp.dot(h1b, wr_ref[...], preferred_element_type=F32) + br_ref[...]


def _mixer_tail(on, z, h, wa, wc, wo, conv_w, gate_b, ln_g, ln_b, wr, br, seq, alpha):
    n, d = h.shape
    tm = LN_ROWS
    halo_blocks = tm // BF16_SUBLANES

    def col(c):
        return pl.BlockSpec((tm, d), lambda i: (i, c))

    def halo(c):
        return pl.BlockSpec((BF16_SUBLANES, d), lambda i: (jnp.maximum(i * halo_blocks - 1, 0), c))

    def full(shape):
        return pl.BlockSpec(shape, lambda i: (0,) * len(shape))

    row = pl.BlockSpec((tm, d), lambda i: (i, 0))
    return pl.pallas_call(
        functools.partial(_mixer_tail_kernel, seq=seq, alpha=alpha),
        grid=(n // tm,),
        in_specs=[row, col(COL_CB), col(COL_CC), col(COL_CX), col(COL_GA), col(COL_GC),
                  halo(COL_CC), halo(COL_CX), row,
                  full((d, d)), full((d, d)), full((d, d)), full((CONV_K, d)), full((2, d)),
                  full((1, d)), full((1, d)), full((d, ROUTER_LANES)), full((1, ROUTER_LANES))],
        out_specs=[row, row, pl.BlockSpec((tm, ROUTER_LANES), lambda i: (i, 0))],
        out_shape=[jax.ShapeDtypeStruct((n, d), F32), jax.ShapeDtypeStruct((n, d), BF16),
                   jax.ShapeDtypeStruct((n, ROUTER_LANES), F32)],
        compiler_params=_cparams(1),
        name="mixer_tail",
    )(on, z, z, z, z, z, z, z, h, wa, wc, wo, conv_w, gate_b,
      ln_g.reshape(1, d), ln_b.reshape(1, d), wr, br)


def _route(logits, n_blk):
    n_tok = logits.shape[0]
    n_asg = n_tok * TOP_K
    gl = logits[:, :N_GROUPS]
    g_max = jnp.max(gl, axis=-1, keepdims=True)
    g_idx = jnp.argmax(gl, axis=-1)[:, None].astype(jnp.int32)
    g_p = 1.0 / jnp.sum(jnp.exp(gl - g_max), axis=-1, keepdims=True)
    e_all = logits[:, N_GROUPS:N_GROUPS + N_EXPERTS].reshape(n_tok, N_GROUPS, EXPERTS_PER_GROUP)
    in_group = g_idx[:, :, None] == jnp.arange(N_GROUPS, dtype=jnp.int32)[None, :, None]
    e_logits = jnp.sum(jnp.where(in_group, e_all, 0.0), axis=1)
    loc = jnp.arange(EXPERTS_PER_GROUP, dtype=jnp.int32)[None, :]
    i0 = jnp.argmax(e_logits, axis=-1)[:, None].astype(jnp.int32)
    e0 = jnp.max(e_logits, axis=-1, keepdims=True)
    rest = jnp.where(loc == i0, -jnp.inf, e_logits)
    i1 = jnp.argmax(rest, axis=-1)[:, None].astype(jnp.int32)
    e1 = jnp.max(rest, axis=-1, keepdims=True)
    r = jnp.exp(e1 - e0)
    gate = jnp.concatenate([g_p / (1.0 + r), g_p * r / (1.0 + r)], axis=-1)
    e_loc = jnp.concatenate([i0, i1], axis=-1)
    expert = (g_idx * EXPERTS_PER_GROUP + e_loc).reshape(-1).astype(jnp.int32)

    onehot = (expert[:, None] == jnp.arange(N_EXPERTS, dtype=jnp.int32)[None, :]).astype(jnp.int32)
    csum = jnp.cumsum(onehot, axis=0)
    sizes = csum[-1]
    padded = (sizes + EXPERT_ROWS - 1) // EXPERT_ROWS * EXPERT_ROWS
    pad_end = jnp.cumsum(padded)
    pad_start = pad_end - padded
    dest = jnp.sum(onehot * (csum - 1 + pad_start[None, :]), axis=1)
    src_tok = jnp.zeros((n_blk * EXPERT_ROWS,), jnp.int32).at[dest].set(
        jnp.arange(n_asg, dtype=jnp.int32) // TOP_K)
    blk_e = jnp.minimum(jnp.searchsorted(pad_end, jnp.arange(n_blk, dtype=jnp.int32) * EXPERT_ROWS,
                                         side='right'), N_EXPERTS - 1).astype(jnp.int32)
    n_used = (pad_end[-1] // EXPERT_ROWS).astype(jnp.int32).reshape(1)
    return gate, dest.reshape(n_tok, TOP_K), src_tok, blk_e, n_used


def _expert_kernel(blk_e_ref, n_used_ref, x_ref, w1_ref, w3_ref, w2_ref, y_ref):
    i = pl.program_id(0)

    @pl.when(i < n_used_ref[0])
    def _():
        x = x_ref[...]
        a = jnp.dot(x, w1_ref[...], preferred_element_type=F32)
        b = jnp.dot(x, w3_ref[...], preferred_element_type=F32)
        mid = (a * jax.nn.sigmoid(a) * b).astype(BF16)
        y_ref[...] = jnp.dot(mid, w2_ref[...], preferred_element_type=F32).astype(y_ref.dtype)

    @pl.when(i >= n_used_ref[0])
    def _():
        y_ref[...] = jnp.zeros(y_ref.shape, y_ref.dtype)


def _expert_mlp(xbuf, blk_e, n_used, w1, w3, w2):
    n_rows, d = xbuf.shape
    de = w1.shape[-1]
    n_blk = n_rows // EXPERT_ROWS
    grid_spec = pltpu.PrefetchScalarGridSpec(
        num_scalar_prefetch=2,
        grid=(n_blk,),
        in_specs=[
            pl.BlockSpec((EXPERT_ROWS, d), lambda i, be, nu: (jnp.minimum(i, nu[0] - 1), 0)),
            pl.BlockSpec((None, d, de), lambda i, be, nu: (be[i], 0, 0)),
            pl.BlockSpec((None, d, de), lambda i, be, nu: (be[i], 0, 0)),
            pl.BlockSpec((None, de, d), lambda i, be, nu: (be[i], 0, 0)),
        ],
        out_specs=pl.BlockSpec((EXPERT_ROWS, d), lambda i, be, nu: (i, 0)),
    )
    return pl.pallas_call(
        _expert_kernel,
        grid_spec=grid_spec,
        out_shape=jax.ShapeDtypeStruct((n_rows, d), BF16),
        compiler_params=_cparams(1),
        name="expert_mlp",
    )(blk_e, n_used, xbuf, w1, w3, w2)


def _combine_kernel(h_ref, y0_ref, y1_ref, gate_ref, g_ref, b_ref, o_ref, ob_ref, *, alpha):
    gate = gate_ref[...]
    ffn = gate[:, 0:1] * y0_ref[...].astype(F32) + gate[:, 1:2] * y1_ref[...].astype(F32)
    h2 = _layer_norm(alpha * h_ref[...] + ffn, g_ref[...], b_ref[...])
    o_ref[...] = h2
    ob_ref[...] = h2.astype(BF16)


def _combine(h, y0, y1, gate, ln_g, ln_b, alpha):
    n, d = h.shape
    row = pl.BlockSpec((LN_ROWS, d), lambda i: (i, 0))
    vec = pl.BlockSpec((1, d), lambda i: (0, 0))
    return pl.pallas_call(
        functools.partial(_combine_kernel, alpha=alpha),
        grid=(n // LN_ROWS,),
        in_specs=[row, row, row, pl.BlockSpec((LN_ROWS, TOP_K), lambda i: (i, 0)), vec, vec],
        out_specs=[row, row],
        out_shape=[jax.ShapeDtypeStruct((n, d), F32), jax.ShapeDtypeStruct((n, d), BF16)],
        compiler_params=_cparams(1),
        name="moe_combine",
    )(h, y0, y1, gate, ln_g.reshape(1, d), ln_b.reshape(1, d))


def kernel(x, ln0_g, ln0_b, rel_table, w_in, gate_b, lam_vecs, subln_g, w_attn_proj, conv_w,
           w_conv_proj, w_out, ln1_g, ln1_b, w_rg, b_rg, w_re, b_re, w1, w3, w2, ln2_g, ln2_b):
    batch, seq, d = x.shape
    depth = w_in.shape[0]
    n_tok = batch * seq
    alpha = (2 * depth) ** 0.25
    assert seq % ATTN_T == 0 and n_tok % PROJ_TM == 0 and seq % LN_ROWS == 0
    n_asg = n_tok * TOP_K
    n_blk = -(-(n_asg + N_EXPERTS * (EXPERT_ROWS - 1)) // EXPERT_ROWS)

    bias, far_bias = _bias_tiles(rel_table, ATTN_T)
    h, hb = _input_layer_norm(x.reshape(n_tok, d), ln0_g, ln0_b)
    for l in range(depth):
        lam_init = 0.8 - 0.6 * math.exp(-0.3 * l)
        lv = lam_vecs[l].astype(F32)
        lam = jnp.exp(jnp.sum(lv[0] * lv[1])) - jnp.exp(jnp.sum(lv[2] * lv[3])) + lam_init

        w_l = w_in[l]
        w_tok = jnp.concatenate([w_l[:, :IN_V * d], w_l[:, IN_CB * d:]], axis=1).astype(BF16)
        w_vt = w_l[:, IN_V * d:IN_CB * d].T.astype(BF16)
        z = _in_projection(hb, w_tok)
        vt = _v_projection_t(hb, w_vt)
        on = _diff_attention(z, vt, lam, bias, far_bias, subln_g[l], batch, 1.0 - lam_init)
        w_router = jnp.zeros((d, ROUTER_LANES), F32)
        w_router = w_router.at[:, :N_GROUPS].set(w_rg[l]).at[:, N_GROUPS:N_GROUPS + N_EXPERTS].set(w_re[l])
        b_router = jnp.zeros((1, ROUTER_LANES), F32)
        b_router = b_router.at[0, :N_GROUPS].set(b_rg[l]).at[0, N_GROUPS:N_GROUPS + N_EXPERTS].set(b_re[l])
        h1, h1b, logits = _mixer_tail(
            on, z, h, w_attn_proj[l].astype(BF16), w_conv_proj[l].astype(BF16), w_out[l].astype(BF16),
            conv_w[l], gate_b[l], ln1_g[l], ln1_b[l], w_router.astype(BF16), b_router, seq, alpha)

        gate, dest, src_tok, blk_e, n_used = _route(logits, n_blk)
        xbuf = h1b[src_tok]
        ybuf = _expert_mlp(xbuf, blk_e, n_used, w1[l].astype(BF16), w3[l].astype(BF16),
                           w2[l].astype(BF16))
        h, hb = _combine(h1, ybuf[dest[:, 0]], ybuf[dest[:, 1]], gate, ln2_g[l], ln2_b[l], alpha)
    return h.reshape(batch, seq, d)
```

```python
import functools
import math

import jax
import jax.numpy as jnp
from jax import lax
from jax.experimental import pallas as pl
from jax.experimental.pallas import tpu as pltpu

F32 = jnp.float32
BF16 = jnp.bfloat16

N_HEADS = 8
HEAD_DIM = 64
V_DIM = 2 * HEAD_DIM
N_BUCKETS = 32
MAX_DISTANCE = 128
CONV_K = 3
N_GROUPS = 4
EXPERTS_PER_GROUP = 8
N_EXPERTS = N_GROUPS * EXPERTS_PER_GROUP
TOP_K = 2
LN_EPS = 1e-5
MASK_VALUE = -1e30
LOG2_E = math.log2(math.e)

IN_Q, IN_K, IN_V, IN_CB, IN_CC, IN_CX, IN_GA, IN_GC = range(8)
COL_Q, COL_K, COL_CB, COL_CC, COL_CX, COL_GA, COL_GC = range(7)

LANES = 128
BF16_SUBLANES = 16
ROUTER_LANES = LANES

LN_ROWS = 512
PROJ_TM = 1024
PROJ_TN = 1024
ATTN_T = 512
EXPERT_ROWS = 512
VMEM_LIMIT = 56 * 1024 * 1024


def _cparams(n_axes):
    return pltpu.CompilerParams(dimension_semantics=("arbitrary",) * n_axes,
                                vmem_limit_bytes=VMEM_LIMIT)


def _layer_norm(x, g, b):
    mu = jnp.mean(x, axis=-1, keepdims=True)
    xc = x - mu
    var = jnp.mean(xc * xc, axis=-1, keepdims=True)
    return xc * lax.rsqrt(var + LN_EPS) * g + b


def _ln_kernel(x_ref, g_ref, b_ref, h_ref, hb_ref):
    h = _layer_norm(x_ref[...], g_ref[...], b_ref[...])
    h_ref[...] = h
    hb_ref[...] = h.astype(BF16)


def _input_layer_norm(x, g, b):
    n, d = x.shape
    row = pl.BlockSpec((LN_ROWS, d), lambda i: (i, 0))
    vec = pl.BlockSpec((1, d), lambda i: (0, 0))
    return pl.pallas_call(
        _ln_kernel,
        grid=(n // LN_ROWS,),
        in_specs=[row, vec, vec],
        out_specs=[row, row],
        out_shape=[jax.ShapeDtypeStruct((n, d), F32), jax.ShapeDtypeStruct((n, d), BF16)],
        compiler_params=_cparams(1),
        name="input_layer_norm",
    )(x, g.reshape(1, d), b.reshape(1, d))


def _matmul_kernel(x_ref, w_ref, o_ref):
    o_ref[...] = jnp.dot(x_ref[...], w_ref[...], preferred_element_type=F32).astype(o_ref.dtype)


def _in_projection(hb, w):
    n, d = hb.shape
    c = w.shape[1]
    return pl.pallas_call(
        _matmul_kernel,
        grid=(n // PROJ_TM, c // PROJ_TN),
        in_specs=[pl.BlockSpec((PROJ_TM, d), lambda i, j: (i, 0)),
                  pl.BlockSpec((d, PROJ_TN), lambda i, j: (0, j))],
        out_specs=pl.BlockSpec((PROJ_TM, PROJ_TN), lambda i, j: (i, j)),
        out_shape=jax.ShapeDtypeStruct((n, c), BF16),
        compiler_params=_cparams(2),
        name="in_projection",
    )(hb, w)


def _matmul_nt_kernel(w_ref, x_ref, o_ref):
    o_ref[...] = lax.dot_general(w_ref[...], x_ref[...], (((1,), (1,)), ((), ())),
                                 preferred_element_type=F32).astype(o_ref.dtype)


def _v_projection_t(hb, wvt):
    n, d = hb.shape
    c = wvt.shape[0]
    t = ATTN_T
    return pl.pallas_call(
        _matmul_nt_kernel,
        grid=(n // t,),
        in_specs=[pl.BlockSpec((c, d), lambda i: (0, 0)),
                  pl.BlockSpec((t, d), lambda i: (i, 0))],
        out_specs=pl.BlockSpec((None, c, t), lambda i: (i, 0, 0)),
        out_shape=jax.ShapeDtypeStruct((n // t, c, t), BF16),
        compiler_params=_cparams(1),
        name="v_projection_t",
    )(wvt, hb)


def _rel_bucket(rel):
    n = jnp.maximum(rel, 0)
    max_exact = N_BUCKETS // 2
    nf = jnp.maximum(n, 1).astype(F32)
    large = max_exact + (jnp.log(nf / max_exact) / math.log(MAX_DISTANCE / max_exact)
                         * (N_BUCKETS - max_exact)).astype(jnp.int32)
    large = jnp.minimum(large, N_BUCKETS - 1)
    return jnp.where(n < max_exact, n, large)


def _bias_tiles(rel_table, t):
    assert t >= MAX_DISTANCE
    rel = jnp.arange(t, dtype=jnp.int32)[None, :] - jnp.arange(t, dtype=jnp.int32)[:, None]
    table = rel_table.astype(F32) * LOG2_E
    diag = jnp.where((rel >= 0)[:, :, None], table[_rel_bucket(rel)], MASK_VALUE)
    prev = table[_rel_bucket(rel + t)]
    tiles = jnp.stack([diag, prev], axis=0)
    return jnp.transpose(tiles, (3, 0, 1, 2)), table[N_BUCKETS - 1]


def _attn_kernel(lam_ref, far_ref, q_ref, k_ref, vt_ref, bias_ref, g_ref, o_ref,
                 m_ref, acc_ref, sa_ref, sb_ref, *, t, post_scale):
    head = pl.program_id(0)
    qi = pl.program_id(2)
    q = (q_ref[...].astype(F32) * (HEAD_DIM ** -0.5 * LOG2_E)).astype(BF16)
    lane = lax.broadcasted_iota(jnp.int32, q.shape, 1)
    zero = jnp.zeros_like(q)
    q_halves = (jnp.where(lane < HEAD_DIM, q, zero), jnp.where(lane >= HEAD_DIM, q, zero))
    ones = jnp.ones((BF16_SUBLANES, t), BF16)

    m_ref[...] = jnp.full(m_ref.shape, MASK_VALUE, F32)
    acc_ref[...] = jnp.zeros(acc_ref.shape, F32)

    def scores(j, s_ref):
        k = k_ref[pl.ds(pl.multiple_of(j * t, t), t), :]
        for br in range(2):
            s_ref[br] = lax.dot_general(k, q_halves[br], (((1,), (1,)), ((), ())),
                                        preferred_element_type=F32)

    def softmax_pv(j, s_ref, bias, shift):
        v_aug = jnp.concatenate([vt_ref[j], ones], axis=0)
        for br in range(2):
            s = s_ref[br]
            if bias is not None:
                s = s + bias
            m_prev = m_ref[br]
            m_tile = jnp.max(s, axis=0, keepdims=True)
            if shift is not None:
                m_tile = m_tile + shift
            m_new = jnp.maximum(m_prev, m_tile)
            alpha = jnp.exp2(m_prev - m_new)
            p = jnp.exp2(s - (m_new if shift is None else m_new - shift)).astype(BF16)
            acc_ref[br] = alpha * acc_ref[br] + jnp.dot(v_aug, p, preferred_element_type=F32)
            m_ref[br] = m_new

    far_bias = far_ref[head]

    @pl.when(qi >= 1)
    def _():
        n_far = qi - 1
        odd = n_far % 2

        @pl.when(odd == 1)
        def _():
            scores(0, sb_ref)
            softmax_pv(0, sb_ref, None, far_bias)

        scores(odd, sa_ref)

        def pair(i, carry):
            j = odd + 2 * i
            scores(j + 1, sb_ref)
            softmax_pv(j, sa_ref, None, far_bias)
            scores(j + 2, sa_ref)
            softmax_pv(j + 1, sb_ref, None, far_bias)
            return carry

        lax.fori_loop(0, n_far // 2, pair, 0)

    scores(qi, sb_ref)

    @pl.when(qi >= 1)
    def _():
        softmax_pv(qi - 1, sa_ref, bias_ref[1], None)

    softmax_pv(qi, sb_ref, bias_ref[0], None)

    a1 = acc_ref[0]
    a2 = acc_ref[1]
    o = a1[:V_DIM] / a1[V_DIM:V_DIM + 1] - lam_ref[0] * (a2[:V_DIM] / a2[V_DIM:V_DIM + 1])
    ms = jnp.mean(o * o, axis=0, keepdims=True)
    on = o * lax.rsqrt(ms + LN_EPS) * (g_ref[...] * post_scale)
    o_ref[...] = on.T.astype(o_ref.dtype)


def _diff_attention(z, vt, lam, bias, far_bias, subln_g, batch, post_scale):
    n = z.shape[0]
    seq = n // batch
    t = ATTN_T
    nq = seq // t
    d_attn = N_HEADS * V_DIM
    smem = pl.BlockSpec(memory_space=pltpu.SMEM)
    return pl.pallas_call(
        functools.partial(_attn_kernel, t=t, post_scale=post_scale),
        grid=(N_HEADS, batch, nq),
        in_specs=[
            smem, smem,
            pl.BlockSpec((t, V_DIM), lambda h, b, i: (b * nq + i, COL_Q * N_HEADS + h)),
            pl.BlockSpec((seq, V_DIM), lambda h, b, i: (b, COL_K * N_HEADS + h)),
            pl.BlockSpec((nq, V_DIM, t), lambda h, b, i: (b, h, 0)),
            pl.BlockSpec((None, 2, t, t), lambda h, b, i: (h, 0, 0, 0)),
            pl.BlockSpec((V_DIM, 1), lambda h, b, i: (0, 0)),
        ],
        out_specs=pl.BlockSpec((t, V_DIM), lambda h, b, i: (b * nq + i, h)),
        out_shape=jax.ShapeDtypeStruct((n, d_attn), BF16),
        scratch_shapes=[pltpu.VMEM((2, 1, t), F32),
                        pltpu.VMEM((2, V_DIM + BF16_SUBLANES, t), F32),
                        pltpu.VMEM((2, t, t), F32), pltpu.VMEM((2, t, t), F32)],
        compiler_params=_cparams(3),
        name="diff_attention",
    )(lam.reshape(1), far_bias, z, z, vt, bias, subln_g.reshape(V_DIM, 1))


def _mixer_tail_kernel(on_ref, cb_ref, cc_ref, cx_ref, ga_ref, gc_ref, cch_ref, cxh_ref, h_ref,
                       wa_ref, wc_ref, wo_ref, convw_ref, gb_ref, lg_ref, lb_ref, wr_ref, br_ref,
                       h1_ref, h1b_ref, logit_ref, *, seq, alpha):
    i = pl.program_id(0)
    tm = on_ref.shape[0]
    attn = jnp.dot(on_ref[...], wa_ref[...], preferred_element_type=F32)

    u = cc_ref[...].astype(F32) * cx_ref[...].astype(F32)
    halo = cch_ref[...].astype(F32) * cxh_ref[...].astype(F32)
    halo = jnp.where((i * tm) % seq == 0, 0.0, halo)
    hm1 = halo[BF16_SUBLANES - 1:BF16_SUBLANES]
    hm2 = halo[BF16_SUBLANES - 2:BF16_SUBLANES - 1]
    row = lax.broadcasted_iota(jnp.int32, u.shape, 0)
    u1 = jnp.where(row == 0, hm1, pltpu.roll(u, 1, axis=0))
    u2 = jnp.where(row == 0, hm2, jnp.where(row == 1, hm1, pltpu.roll(u, 2, axis=0)))
    w = convw_ref[...]
    y = w[2:3] * u + w[1:2] * u1 + w[0:1] * u2
    conv = jnp.dot((cb_ref[...].astype(F32) * y).astype(BF16), wc_ref[...],
                   preferred_element_type=F32)

    gb = gb_ref[...]
    merged = (jax.nn.sigmoid(ga_ref[...].astype(F32) + gb[0:1]) * attn
              + jax.nn.sigmoid(gc_ref[...].astype(F32) + gb[1:2]) * conv)
    mix = jnp.dot(merged.astype(BF16), wo_ref[...], preferred_element_type=F32)
    h1 = _layer_norm(alpha * h_ref[...] + mix, lg_ref[...], lb_ref[...])
    h1b = h1.astype(BF16)
    h1_ref[...] = h1
    h1b_ref[...] = h1b
    logit_ref[...] = jnp.dot(h1b, wr_ref[...], preferred_element_type=F32) + br_ref[...]


def _mixer_tail(on, z, h, wa, wc, wo, conv_w, gate_b, ln_g, ln_b, wr, br, seq, alpha):
    n, d = h.shape
    tm = LN_ROWS
    halo_blocks = tm // BF16_SUBLANES

    def col(c):
        return pl.BlockSpec((tm, d), lambda i: (i, c))

    def halo(c):
        return pl.BlockSpec((BF16_SUBLANES, d), lambda i: (jnp.maximum(i * halo_blocks - 1, 0), c))

    def full(shape):
        return pl.BlockSpec(shape, lambda i: (0,) * len(shape))

    row = pl.BlockSpec((tm, d), lambda i: (i, 0))
    return pl.pallas_call(
        functools.partial(_mixer_tail_kernel, seq=seq, alpha=alpha),
        grid=(n // tm,),
        in_specs=[row, col(COL_CB), col(COL_CC), col(COL_CX), col(COL_GA), col(COL_GC),
                  halo(COL_CC), halo(COL_CX), row,
                  full((d, d)), full((d, d)), full((d, d)), full((CONV_K, d)), full((2, d)),
                  full((1, d)), full((1, d)), full((d, ROUTER_LANES)), full((1, ROUTER_LANES))],
        out_specs=[row, row, pl.BlockSpec((tm, ROUTER_LANES), lambda i: (i, 0))],
        out_shape=[jax.ShapeDtypeStruct((n, d), F32), jax.ShapeDtypeStruct((n, d), BF16),
                   jax.ShapeDtypeStruct((n, ROUTER_LANES), F32)],
        compiler_params=_cparams(1),
        name="mixer_tail",
    )(on, z, z, z, z, z, z, z, h, wa, wc, wo, conv_w, gate_b,
      ln_g.reshape(1, d), ln_b.reshape(1, d), wr, br)


def _route(logits, n_blk):
    n_tok = logits.shape[0]
    n_asg = n_tok * TOP_K
    gl = logits[:, :N_GROUPS]
    g_max = jnp.max(gl, axis=-1, keepdims=True)
    g_idx = jnp.argmax(gl, axis=-1)[:, None].astype(jnp.int32)
    g_p = 1.0 / jnp.sum(jnp.exp(gl - g_max), axis=-1, keepdims=True)
    e_all = logits[:, N_GROUPS:N_GROUPS + N_EXPERTS].reshape(n_tok, N_GROUPS, EXPERTS_PER_GROUP)
    in_group = g_idx[:, :, None] == jnp.arange(N_GROUPS, dtype=jnp.int32)[None, :, None]
    e_logits = jnp.sum(jnp.where(in_group, e_all, 0.0), axis=1)
    loc = jnp.arange(EXPERTS_PER_GROUP, dtype=jnp.int32)[None, :]
    i0 = jnp.argmax(e_logits, axis=-1)[:, None].astype(jnp.int32)
    e0 = jnp.max(e_logits, axis=-1, keepdims=True)
    rest = jnp.where(loc == i0, -jnp.inf, e_logits)
    i1 = jnp.argmax(rest, axis=-1)[:, None].astype(jnp.int32)
    e1 = jnp.max(rest, axis=-1, keepdims=True)
    r = jnp.exp(e1 - e0)
    gate = jnp.concatenate([g_p / (1.0 + r), g_p * r / (1.0 + r)], axis=-1)
    e_loc = jnp.concatenate([i0, i1], axis=-1)
    expert = (g_idx * EXPERTS_PER_GROUP + e_loc).reshape(-1).astype(jnp.int32)

    onehot = (expert[:, None] == jnp.arange(N_EXPERTS, dtype=jnp.int32)[None, :]).astype(jnp.int32)
    csum = jnp.cumsum(onehot, axis=0)
    sizes = csum[-1]
    padded = (sizes + EXPERT_ROWS - 1) // EXPERT_ROWS * EXPERT_ROWS
    pad_end = jnp.cumsum(padded)
    pad_start = pad_end - padded
    dest = jnp.sum(onehot * (csum - 1 + pad_start[None, :]), axis=1)
    src_tok = jnp.zeros((n_blk * EXPERT_ROWS,), jnp.int32).at[dest].set(
        jnp.arange(n_asg, dtype=jnp.int32) // TOP_K)
    blk_e = jnp.minimum(jnp.searchsorted(pad_end, jnp.arange(n_blk, dtype=jnp.int32) * EXPERT_ROWS,
                                         side='right'), N_EXPERTS - 1).astype(jnp.int32)
    n_used = (pad_end[-1] // EXPERT_ROWS).astype(jnp.int32).reshape(1)
    return gate, dest.reshape(n_tok, TOP_K), src_tok, blk_e, n_used


def _expert_kernel(blk_e_ref, n_used_ref, x_ref, w1_ref, w3_ref, w2_ref, y_ref):
    i = pl.program_id(0)

    @pl.when(i < n_used_ref[0])
    def _():
        x = x_ref[...]
        a = jnp.dot(x, w1_ref[...], preferred_element_type=F32)
        b = jnp.dot(x, w3_ref[...], preferred_element_type=F32)
        mid = (a * jax.nn.sigmoid(a) * b).astype(BF16)
        y_ref[...] = jnp.dot(mid, w2_ref[...], preferred_element_type=F32).astype(y_ref.dtype)

    @pl.when(i >= n_used_ref[0])
    def _():
        y_ref[...] = jnp.zeros(y_ref.shape, y_ref.dtype)


def _expert_mlp(xbuf, blk_e, n_used, w1, w3, w2):
    n_rows, d = xbuf.shape
    de = w1.shape[-1]
    n_blk = n_rows // EXPERT_ROWS
    grid_spec = pltpu.PrefetchScalarGridSpec(
        num_scalar_prefetch=2,
        grid=(n_blk,),
        in_specs=[
            pl.BlockSpec((EXPERT_ROWS, d), lambda i, be, nu: (jnp.minimum(i, nu[0] - 1), 0)),
            pl.BlockSpec((None, d, de), lambda i, be, nu: (be[i], 0, 0)),
            pl.BlockSpec((None, d, de), lambda i, be, nu: (be[i], 0, 0)),
            pl.BlockSpec((None, de, d), lambda i, be, nu: (be[i], 0, 0)),
        ],
        out_specs=pl.BlockSpec((EXPERT_ROWS, d), lambda i, be, nu: (i, 0)),
    )
    return pl.pallas_call(
        _expert_kernel,
        grid_spec=grid_spec,
        out_shape=jax.ShapeDtypeStruct((n_rows, d), BF16),
        compiler_params=_cparams(1),
        name="expert_mlp",
    )(blk_e, n_used, xbuf, w1, w3, w2)


def _combine_kernel(h_ref, y0_ref, y1_ref, gate_ref, g_ref, b_ref, o_ref, ob_ref, *, alpha):
    gate = gate_ref[...]
    ffn = gate[:, 0:1] * y0_ref[...].astype(F32) + gate[:, 1:2] * y1_ref[...].astype(F32)
    h2 = _layer_norm(alpha * h_ref[...] + ffn, g_ref[...], b_ref[...])
    o_ref[...] = h2
    ob_ref[...] = h2.astype(BF16)


def _combine(h, y_pair, gate, ln_g, ln_b, alpha):
    n, d = h.shape
    row = pl.BlockSpec((LN_ROWS, d), lambda i: (i, 0))
    vec = pl.BlockSpec((1, d), lambda i: (0, 0))
    return pl.pallas_call(
        functools.partial(_combine_kernel, alpha=alpha),
        grid=(n // LN_ROWS,),
        in_specs=[row, row, pl.BlockSpec((LN_ROWS, d), lambda i: (i, 1)),
                  pl.BlockSpec((LN_ROWS, TOP_K), lambda i: (i, 0)), vec, vec],
        out_specs=[row, row],
        out_shape=[jax.ShapeDtypeStruct((n, d), F32), jax.ShapeDtypeStruct((n, d), BF16)],
        compiler_params=_cparams(1),
        name="moe_combine",
    )(h, y_pair, y_pair, gate, ln_g.reshape(1, d), ln_b.reshape(1, d))


def kernel(x, ln0_g, ln0_b, rel_table, w_in, gate_b, lam_vecs, subln_g, w_attn_proj, conv_w,
           w_conv_proj, w_out, ln1_g, ln1_b, w_rg, b_rg, w_re, b_re, w1, w3, w2, ln2_g, ln2_b):
    batch, seq, d = x.shape
    depth = w_in.shape[0]
    n_tok = batch * seq
    alpha = (2 * depth) ** 0.25
    assert seq % ATTN_T == 0 and n_tok % PROJ_TM == 0 and seq % LN_ROWS == 0
    n_asg = n_tok * TOP_K
    n_blk = -(-(n_asg + N_EXPERTS * (EXPERT_ROWS - 1)) // EXPERT_ROWS)

    bias, far_bias = _bias_tiles(rel_table, ATTN_T)
    h, hb = _input_layer_norm(x.reshape(n_tok, d), ln0_g, ln0_b)
    for l in range(depth):
        lam_init = 0.8 - 0.6 * math.exp(-0.3 * l)
        lv = lam_vecs[l].astype(F32)
        lam = jnp.exp(jnp.sum(lv[0] * lv[1])) - jnp.exp(jnp.sum(lv[2] * lv[3])) + lam_init

        w_l = w_in[l]
        w_tok = jnp.concatenate([w_l[:, :IN_V * d], w_l[:, IN_CB * d:]], axis=1).astype(BF16)
        w_vt = w_l[:, IN_V * d:IN_CB * d].T.astype(BF16)
        z = _in_projection(hb, w_tok)
        vt = _v_projection_t(hb, w_vt)
        on = _diff_attention(z, vt, lam, bias, far_bias, subln_g[l], batch, 1.0 - lam_init)
        w_router = jnp.zeros((d, ROUTER_LANES), F32)
        w_router = w_router.at[:, :N_GROUPS].set(w_rg[l]).at[:, N_GROUPS:N_GROUPS + N_EXPERTS].set(w_re[l])
        b_router = jnp.zeros((1, ROUTER_LANES), F32)
        b_router = b_router.at[0, :N_GROUPS].set(b_rg[l]).at[0, N_GROUPS:N_GROUPS + N_EXPERTS].set(b_re[l])
        h1, h1b, logits = _mixer_tail(
            on, z, h, w_attn_proj[l].astype(BF16), w_conv_proj[l].astype(BF16), w_out[l].astype(BF16),
            conv_w[l], gate_b[l], ln1_g[l], ln1_b[l], w_router.astype(BF16), b_router, seq, alpha)

        gate, dest, src_tok, blk_e, n_used = _route(logits, n_blk)
        xbuf = h1b[src_tok]
        ybuf = _expert_mlp(xbuf, blk_e, n_used, w1[l].astype(BF16), w3[l].astype(BF16),
                           w2[l].astype(BF16))
        y_pair = ybuf[dest.reshape(-1)].reshape(n_tok, TOP_K * d)
        h, hb = _combine(h1, y_pair, gate, ln2_g[l], ln2_b[l], alpha)
    return h.reshape(batch, seq, d)
```

```python
import functools
import math

import jax
import jax.numpy as jnp
from jax import lax
from jax.experimental import pallas as pl
from jax.experimental.pallas import tpu as pltpu

F32 = jnp.float32
BF16 = jnp.bfloat16

N_HEADS = 8
HEAD_DIM = 64
V_DIM = 2 * HEAD_DIM
N_BUCKETS = 32
MAX_DISTANCE = 128
CONV_K = 3
N_GROUPS = 4
EXPERTS_PER_GROUP = 8
N_EXPERTS = N_GROUPS * EXPERTS_PER_GROUP
TOP_K = 2
LN_EPS = 1e-5
MASK_VALUE = -1e30
LOG2_E = math.log2(math.e)

IN_Q, IN_K, IN_V, IN_CB, IN_CC, IN_CX, IN_GA, IN_GC = range(8)
COL_Q, COL_K, COL_CB, COL_CC, COL_CX, COL_GA, COL_GC = range(7)

LANES = 128
BF16_SUBLANES = 16
ROUTER_LANES = LANES

LN_ROWS = 512
PROJ_TM = 1024
PROJ_TN = 1024
ATTN_T = 512
EXPERT_ROWS = 512
VMEM_LIMIT = 56 * 1024 * 1024


def _cparams(n_axes):
    return pltpu.CompilerParams(dimension_semantics=("arbitrary",) * n_axes,
                                vmem_limit_bytes=VMEM_LIMIT)


def _layer_norm(x, g, b):
    mu = jnp.mean(x, axis=-1, keepdims=True)
    xc = x - mu
    var = jnp.mean(xc * xc, axis=-1, keepdims=True)
    return xc * lax.rsqrt(var + LN_EPS) * g + b


def _ln_kernel(x_ref, g_ref, b_ref, h_ref, hb_ref):
    h = _layer_norm(x_ref[...], g_ref[...], b_ref[...])
    h_ref[...] = h
    hb_ref[...] = h.astype(BF16)


def _input_layer_norm(x, g, b):
    n, d = x.shape
    row = pl.BlockSpec((LN_ROWS, d), lambda i: (i, 0))
    vec = pl.BlockSpec((1, d), lambda i: (0, 0))
    return pl.pallas_call(
        _ln_kernel,
        grid=(n // LN_ROWS,),
        in_specs=[row, vec, vec],
        out_specs=[row, row],
        out_shape=[jax.ShapeDtypeStruct((n, d), F32), jax.ShapeDtypeStruct((n, d), BF16)],
        compiler_params=_cparams(1),
        name="input_layer_norm",
    )(x, g.reshape(1, d), b.reshape(1, d))


def _matmul_kernel(x_ref, w_ref, o_ref):
    o_ref[...] = jnp.dot(x_ref[...], w_ref[...], preferred_element_type=F32).astype(o_ref.dtype)


def _in_projection(hb, w):
    n, d = hb.shape
    c = w.shape[1]
    return pl.pallas_call(
        _matmul_kernel,
        grid=(n // PROJ_TM, c // PROJ_TN),
        in_specs=[pl.BlockSpec((PROJ_TM, d), lambda i, j: (i, 0)),
                  pl.BlockSpec((d, PROJ_TN), lambda i, j: (0, j))],
        out_specs=pl.BlockSpec((PROJ_TM, PROJ_TN), lambda i, j: (i, j)),
        out_shape=jax.ShapeDtypeStruct((n, c), BF16),
        compiler_params=_cparams(2),
        name="in_projection",
    )(hb, w)


def _matmul_nt_kernel(w_ref, x_ref, o_ref):
    o_ref[...] = lax.dot_general(w_ref[...], x_ref[...], (((1,), (1,)), ((), ())),
                                 preferred_element_type=F32).astype(o_ref.dtype)


def _v_projection_t(hb, wvt):
    n, d = hb.shape
    c = wvt.shape[0]
    t = ATTN_T
    return pl.pallas_call(
        _matmul_nt_kernel,
        grid=(n // t,),
        in_specs=[pl.BlockSpec((c, d), lambda i: (0, 0)),
                  pl.BlockSpec((t, d), lambda i: (i, 0))],
        out_specs=pl.BlockSpec((None, c, t), lambda i: (i, 0, 0)),
        out_shape=jax.ShapeDtypeStruct((n // t, c, t), BF16),
        compiler_params=_cparams(1),
        name="v_projection_t",
    )(wvt, hb)


def _rel_bucket(rel):
    n = jnp.maximum(rel, 0)
    max_exact = N_BUCKETS // 2
    nf = jnp.maximum(n, 1).astype(F32)
    large = max_exact + (jnp.log(nf / max_exact) / math.log(MAX_DISTANCE / max_exact)
                         * (N_BUCKETS - max_exact)).astype(jnp.int32)
    large = jnp.minimum(large, N_BUCKETS - 1)
    return jnp.where(n < max_exact, n, large)


def _bias_tiles(rel_table, t):
    assert t >= MAX_DISTANCE
    rel = jnp.arange(t, dtype=jnp.int32)[None, :] - jnp.arange(t, dtype=jnp.int32)[:, None]
    table = rel_table.astype(F32) * LOG2_E
    diag = jnp.where((rel >= 0)[:, :, None], table[_rel_bucket(rel)], MASK_VALUE)
    prev = table[_rel_bucket(rel + t)]
    tiles = jnp.stack([diag, prev], axis=0)
    return jnp.transpose(tiles, (3, 0, 1, 2)), table[N_BUCKETS - 1]


def _attn_kernel(lam_ref, far_ref, q_ref, k_ref, vt_ref, bias_ref, g_ref, o_ref,
                 m_ref, acc_ref, sa_ref, sb_ref, *, t, post_scale):
    head = pl.program_id(0)
    qi = pl.program_id(2)
    q = (q_ref[...].astype(F32) * (HEAD_DIM ** -0.5 * LOG2_E)).astype(BF16)
    lane = lax.broadcasted_iota(jnp.int32, q.shape, 1)
    zero = jnp.zeros_like(q)
    q_halves = (jnp.where(lane < HEAD_DIM, q, zero), jnp.where(lane >= HEAD_DIM, q, zero))
    ones = jnp.ones((BF16_SUBLANES, t), BF16)

    m_ref[...] = jnp.full(m_ref.shape, MASK_VALUE, F32)
    acc_ref[...] = jnp.zeros(acc_ref.shape, F32)

    def scores(j, s_ref):
        k = k_ref[pl.ds(pl.multiple_of(j * t, t), t), :]
        for br in range(2):
            s_ref[br] = lax.dot_general(k, q_halves[br], (((1,), (1,)), ((), ())),
                                        preferred_element_type=F32)

    def softmax_pv(j, s_ref, bias, shift):
        v_aug = jnp.concatenate([vt_ref[j], ones], axis=0)
        for br in range(2):
            s = s_ref[br]
            if bias is not None:
                s = s + bias
            m_prev = m_ref[br]
            m_tile = jnp.max(s, axis=0, keepdims=True)
            if shift is not None:
                m_tile = m_tile + shift
            m_new = jnp.maximum(m_prev, m_tile)
            alpha = jnp.exp2(m_prev - m_new)
            p = jnp.exp2(s - (m_new if shift is None else m_new - shift)).astype(BF16)
            acc_ref[br] = alpha * acc_ref[br] + jnp.dot(v_aug, p, preferred_element_type=F32)
            m_ref[br] = m_new

    far_bias = far_ref[head]

    @pl.when(qi >= 1)
    def _():
        n_far = qi - 1
        odd = n_far % 2

        @pl.when(odd == 1)
        def _():
            scores(0, sb_ref)
            softmax_pv(0, sb_ref, None, far_bias)

        scores(odd, sa_ref)

        def pair(i, carry):
            j = odd + 2 * i
            scores(j + 1, sb_ref)
            softmax_pv(j, sa_ref, None, far_bias)
            scores(j + 2, sa_ref)
            softmax_pv(j + 1, sb_ref, None, far_bias)
            return carry

        lax.fori_loop(0, n_far // 2, pair, 0)

    scores(qi, sb_ref)

    @pl.when(qi >= 1)
    def _():
        softmax_pv(qi - 1, sa_ref, bias_ref[1], None)

    softmax_pv(qi, sb_ref, bias_ref[0], None)

    a1 = acc_ref[0]
    a2 = acc_ref[1]
    o = a1[:V_DIM] / a1[V_DIM:V_DIM + 1] - lam_ref[0] * (a2[:V_DIM] / a2[V_DIM:V_DIM + 1])
    ms = jnp.mean(o * o, axis=0, keepdims=True)
    on = o * lax.rsqrt(ms + LN_EPS) * (g_ref[...] * post_scale)
    o_ref[...] = on.T.astype(o_ref.dtype)


def _diff_attention(z, vt, lam, bias, far_bias, subln_g, batch, post_scale):
    n = z.shape[0]
    seq = n // batch
    t = ATTN_T
    nq = seq // t
    d_attn = N_HEADS * V_DIM
    smem = pl.BlockSpec(memory_space=pltpu.SMEM)
    return pl.pallas_call(
        functools.partial(_attn_kernel, t=t, post_scale=post_scale),
        grid=(N_HEADS, batch, nq),
        in_specs=[
            smem, smem,
            pl.BlockSpec((t, V_DIM), lambda h, b, i: (b * nq + i, COL_Q * N_HEADS + h)),
            pl.BlockSpec((seq, V_DIM), lambda h, b, i: (b, COL_K * N_HEADS + h)),
            pl.BlockSpec((nq, V_DIM, t), lambda h, b, i: (b, h, 0)),
            pl.BlockSpec((None, 2, t, t), lambda h, b, i: (h, 0, 0, 0)),
            pl.BlockSpec((V_DIM, 1), lambda h, b, i: (0, 0)),
        ],
        out_specs=pl.BlockSpec((t, V_DIM), lambda h, b, i: (b * nq + i, h)),
        out_shape=jax.ShapeDtypeStruct((n, d_attn), BF16),
        scratch_shapes=[pltpu.VMEM((2, 1, t), F32),
                        pltpu.VMEM((2, V_DIM + BF16_SUBLANES, t), F32),
                        pltpu.VMEM((2, t, t), F32), pltpu.VMEM((2, t, t), F32)],
        compiler_params=_cparams(3),
        name="diff_attention",
    )(lam.reshape(1), far_bias, z, z, vt, bias, subln_g.reshape(V_DIM, 1))


def _mixer_tail_kernel(on_ref, cb_ref, cc_ref, cx_ref, ga_ref, gc_ref, cch_ref, cxh_ref, h_ref,
                       wa_ref, wc_ref, wo_ref, convw_ref, gb_ref, lg_ref, lb_ref, wr_ref, br_ref,
                       h1_ref, h1b_ref, logit_ref, *, seq, alpha):
    i = pl.program_id(0)
    tm = on_ref.shape[0]
    attn = jnp.dot(on_ref[...], wa_ref[...], preferred_element_type=F32)

    u = cc_ref[...].astype(F32) * cx_ref[...].astype(F32)
    halo = cch_ref[...].astype(F32) * cxh_ref[...].astype(F32)
    halo = jnp.where((i * tm) % seq == 0, 0.0, halo)
    hm1 = halo[BF16_SUBLANES - 1:BF16_SUBLANES]
    hm2 = halo[BF16_SUBLANES - 2:BF16_SUBLANES - 1]
    row = lax.broadcasted_iota(jnp.int32, u.shape, 0)
    u1 = jnp.where(row == 0, hm1, pltpu.roll(u, 1, axis=0))
    u2 = jnp.where(row == 0, hm2, jnp.where(row == 1, hm1, pltpu.roll(u, 2, axis=0)))
    w = convw_ref[...]
    y = w[2:3] * u + w[1:2] * u1 + w[0:1] * u2
    conv = jnp.dot((cb_ref[...].astype(F32) * y).astype(BF16), wc_ref[...],
                   preferred_element_type=F32)

    gb = gb_ref[...]
    merged = (jax.nn.sigmoid(ga_ref[...].astype(F32) + gb[0:1]) * attn
              + jax.nn.sigmoid(gc_ref[...].astype(F32) + gb[1:2]) * conv)
    mix = jnp.dot(merged.astype(BF16), wo_ref[...], preferred_element_type=F32)
    h1 = _layer_norm(alpha * h_ref[...] + mix, lg_ref[...], lb_ref[...])
    h1b = h1.astype(BF16)
    h1_ref[...] = h1
    h1b_ref[...] = h1b
    logit_ref[...] = jnp.dot(h1b, wr_ref[...], preferred_element_type=F32) + br_ref[...]


def _mixer_tail(on, z, h, wa, wc, wo, conv_w, gate_b, ln_g, ln_b, wr, br, seq, alpha):
    n, d = h.shape
    tm = LN_ROWS
    halo_blocks = tm // BF16_SUBLANES

    def col(c):
        return pl.BlockSpec((tm, d), lambda i: (i, c))

    def halo(c):
        return pl.BlockSpec((BF16_SUBLANES, d), lambda i: (jnp.maximum(i * halo_blocks - 1, 0), c))

    def full(shape):
        return pl.BlockSpec(shape, lambda i: (0,) * len(shape))

    row = pl.BlockSpec((tm, d), lambda i: (i, 0))
    return pl.pallas_call(
        functools.partial(_mixer_tail_kernel, seq=seq, alpha=alpha),
        grid=(n // tm,),
        in_specs=[row, col(COL_CB), col(COL_CC), col(COL_CX), col(COL_GA), col(COL_GC),
                  halo(COL_CC), halo(COL_CX), row,
                  full((d, d)), full((d, d)), full((d, d)), full((CONV_K, d)), full((2, d)),
                  full((1, d)), full((1, d)), full((d, ROUTER_LANES)), full((1, ROUTER_LANES))],
        out_specs=[row, row, pl.BlockSpec((tm, ROUTER_LANES), lambda i: (i, 0))],
        out_shape=[jax.ShapeDtypeStruct((n, d), F32), jax.ShapeDtypeStruct((n, d), BF16),
                   jax.ShapeDtypeStruct((n, ROUTER_LANES), F32)],
        compiler_params=_cparams(1),
        name="mixer_tail",
    )(on, z, z, z, z, z, z, z, h, wa, wc, wo, conv_w, gate_b,
      ln_g.reshape(1, d), ln_b.reshape(1, d), wr, br)


def _route(logits, n_blk):
    n_tok = logits.shape[0]
    n_asg = n_tok * TOP_K
    gl = logits[:, :N_GROUPS]
    g_max = jnp.max(gl, axis=-1, keepdims=True)
    g_idx = jnp.argmax(gl, axis=-1)[:, None].astype(jnp.int32)
    g_p = 1.0 / jnp.sum(jnp.exp(gl - g_max), axis=-1, keepdims=True)
    e_all = logits[:, N_GROUPS:N_GROUPS + N_EXPERTS].reshape(n_tok, N_GROUPS, EXPERTS_PER_GROUP)
    in_group = g_idx[:, :, None] == jnp.arange(N_GROUPS, dtype=jnp.int32)[None, :, None]
    e_logits = jnp.sum(jnp.where(in_group, e_all, 0.0), axis=1)
    loc = jnp.arange(EXPERTS_PER_GROUP, dtype=jnp.int32)[None, :]
    i0 = jnp.argmax(e_logits, axis=-1)[:, None].astype(jnp.int32)
    e0 = jnp.max(e_logits, axis=-1, keepdims=True)
    rest = jnp.where(loc == i0, -jnp.inf, e_logits)
    i1 = jnp.argmax(rest, axis=-1)[:, None].astype(jnp.int32)
    e1 = jnp.max(rest, axis=-1, keepdims=True)
    r = jnp.exp(e1 - e0)
    gate = jnp.concatenate([g_p / (1.0 + r), g_p * r / (1.0 + r)], axis=-1)
    e_loc = jnp.concatenate([i0, i1], axis=-1)
    expert = (g_idx * EXPERTS_PER_GROUP + e_loc).reshape(-1).astype(jnp.int32)

    onehot = (expert[:, None] == jnp.arange(N_EXPERTS, dtype=jnp.int32)[None, :]).astype(jnp.int32)
    csum = jnp.cumsum(onehot, axis=0)
    sizes = csum[-1]
    padded = (sizes + EXPERT_ROWS - 1) // EXPERT_ROWS * EXPERT_ROWS
    pad_end = jnp.cumsum(padded)
    pad_start = pad_end - padded
    dest = jnp.sum(onehot * (csum - 1 + pad_start[None, :]), axis=1)
    src_tok = (jnp.arange(n_blk * EXPERT_ROWS, dtype=jnp.int32) % n_tok).at[dest].set(
        jnp.arange(n_asg, dtype=jnp.int32) // TOP_K)
    blk_e = jnp.minimum(jnp.searchsorted(pad_end, jnp.arange(n_blk, dtype=jnp.int32) * EXPERT_ROWS,
                                         side='right'), N_EXPERTS - 1).astype(jnp.int32)
    n_used = (pad_end[-1] // EXPERT_ROWS).astype(jnp.int32).reshape(1)
    return gate, dest.reshape(n_tok, TOP_K), src_tok, blk_e, n_used


def _expert_kernel(blk_e_ref, n_used_ref, x_ref, w1_ref, w3_ref, w2_ref, y_ref,
                   w1b_ref, w3b_ref, w2b_ref):
    i = pl.program_id(0)
    new_expert = jnp.logical_or(i == 0, blk_e_ref[i] != blk_e_ref[jnp.maximum(i - 1, 0)])

    @pl.when(jnp.logical_and(i < n_used_ref[0], new_expert))
    def _():
        w1b_ref[...] = w1_ref[...].astype(BF16)
        w3b_ref[...] = w3_ref[...].astype(BF16)
        w2b_ref[...] = w2_ref[...].astype(BF16)

    @pl.when(i < n_used_ref[0])
    def _():
        x = x_ref[...]
        a = jnp.dot(x, w1b_ref[...], preferred_element_type=F32)
        b = jnp.dot(x, w3b_ref[...], preferred_element_type=F32)
        mid = (a * jax.nn.sigmoid(a) * b).astype(BF16)
        y_ref[...] = jnp.dot(mid, w2b_ref[...], preferred_element_type=F32).astype(y_ref.dtype)

    @pl.when(i >= n_used_ref[0])
    def _():
        y_ref[...] = jnp.zeros(y_ref.shape, y_ref.dtype)


def _expert_mlp(xbuf, blk_e, n_used, w1, w3, w2, layer):
    n_rows, d = xbuf.shape
    de = w1.shape[-1]
    n_blk = n_rows // EXPERT_ROWS
    grid_spec = pltpu.PrefetchScalarGridSpec(
        num_scalar_prefetch=2,
        grid=(n_blk,),
        in_specs=[
            pl.BlockSpec((EXPERT_ROWS, d), lambda i, be, nu: (jnp.minimum(i, nu[0] - 1), 0)),
            pl.BlockSpec((None, None, d, de), lambda i, be, nu: (layer, be[i], 0, 0)),
            pl.BlockSpec((None, None, d, de), lambda i, be, nu: (layer, be[i], 0, 0)),
            pl.BlockSpec((None, None, de, d), lambda i, be, nu: (layer, be[i], 0, 0)),
        ],
        out_specs=pl.BlockSpec((EXPERT_ROWS, d), lambda i, be, nu: (i, 0)),
        scratch_shapes=[pltpu.VMEM((d, de), BF16), pltpu.VMEM((d, de), BF16),
                        pltpu.VMEM((de, d), BF16)],
    )
    return pl.pallas_call(
        _expert_kernel,
        grid_spec=grid_spec,
        out_shape=jax.ShapeDtypeStruct((n_rows, d), BF16),
        compiler_params=_cparams(1),
        name="expert_mlp",
    )(blk_e, n_used, xbuf, w1, w3, w2)


def _combine_kernel(h_ref, y0_ref, y1_ref, gate_ref, g_ref, b_ref, o_ref, ob_ref, *, alpha):
    gate = gate_ref[...]
    ffn = gate[:, 0:1] * y0_ref[...].astype(F32) + gate[:, 1:2] * y1_ref[...].astype(F32)
    h2 = _layer_norm(alpha * h_ref[...] + ffn, g_ref[...], b_ref[...])
    o_ref[...] = h2
    ob_ref[...] = h2.astype(BF16)


def _combine(h, y_both, gate, ln_g, ln_b, alpha):
    n, d = h.shape
    n_tiles = n // LN_ROWS
    row = pl.BlockSpec((LN_ROWS, d), lambda i: (i, 0))
    vec = pl.BlockSpec((1, d), lambda i: (0, 0))
    return pl.pallas_call(
        functools.partial(_combine_kernel, alpha=alpha),
        grid=(n // LN_ROWS,),
        in_specs=[row, row, pl.BlockSpec((LN_ROWS, d), lambda i: (i + n_tiles, 0)),
                  pl.BlockSpec((LN_ROWS, TOP_K), lambda i: (i, 0)), vec, vec],
        out_specs=[row, row],
        out_shape=[jax.ShapeDtypeStruct((n, d), F32), jax.ShapeDtypeStruct((n, d), BF16)],
        compiler_params=_cparams(1),
        name="moe_combine",
    )(h, y_both, y_both, gate, ln_g.reshape(1, d), ln_b.reshape(1, d))


def kernel(x, ln0_g, ln0_b, rel_table, w_in, gate_b, lam_vecs, subln_g, w_attn_proj, conv_w,
           w_conv_proj, w_out, ln1_g, ln1_b, w_rg, b_rg, w_re, b_re, w1, w3, w2, ln2_g, ln2_b):
    batch, seq, d = x.shape
    depth = w_in.shape[0]
    n_tok = batch * seq
    alpha = (2 * depth) ** 0.25
    assert seq % ATTN_T == 0 and n_tok % PROJ_TM == 0 and seq % LN_ROWS == 0
    n_asg = n_tok * TOP_K
    n_blk = -(-(n_asg + N_EXPERTS * (EXPERT_ROWS - 1)) // EXPERT_ROWS)

    bias, far_bias = _bias_tiles(rel_table, ATTN_T)
    h, hb = _input_layer_norm(x.reshape(n_tok, d), ln0_g, ln0_b)
    for l in range(depth):
        lam_init = 0.8 - 0.6 * math.exp(-0.3 * l)
        lv = lam_vecs[l].astype(F32)
        lam = jnp.exp(jnp.sum(lv[0] * lv[1])) - jnp.exp(jnp.sum(lv[2] * lv[3])) + lam_init

        w_l = w_in[l]
        w_tok = jnp.concatenate([w_l[:, :IN_V * d], w_l[:, IN_CB * d:]], axis=1).astype(BF16)
        w_vt = w_l[:, IN_V * d:IN_CB * d].T.astype(BF16)
        z = _in_projection(hb, w_tok)
        vt = _v_projection_t(hb, w_vt)
        on = _diff_attention(z, vt, lam, bias, far_bias, subln_g[l], batch, 1.0 - lam_init)
        w_router = jnp.zeros((d, ROUTER_LANES), F32)
        w_router = w_router.at[:, :N_GROUPS].set(w_rg[l]).at[:, N_GROUPS:N_GROUPS + N_EXPERTS].set(w_re[l])
        b_router = jnp.zeros((1, ROUTER_LANES), F32)
        b_router = b_router.at[0, :N_GROUPS].set(b_rg[l]).at[0, N_GROUPS:N_GROUPS + N_EXPERTS].set(b_re[l])
        h1, h1b, logits = _mixer_tail(
            on, z, h, w_attn_proj[l].astype(BF16), w_conv_proj[l].astype(BF16), w_out[l].astype(BF16),
            conv_w[l], gate_b[l], ln1_g[l], ln1_b[l], w_router.astype(BF16), b_router, seq, alpha)

        gate, dest, src_tok, blk_e, n_used = _route(logits, n_blk)
        xbuf = h1b[src_tok]
        ybuf = _expert_mlp(xbuf, blk_e, n_used, w1, w3, w2, l)
        y_both = ybuf[jnp.concatenate([dest[:, 0], dest[:, 1]])]
        h, hb = _combine(h1, y_both, gate, ln2_g[l], ln2_b[l], alpha)
    return h.reshape(batch, seq, d)
```

```python
import functools
import math

import jax
import jax.numpy as jnp
from jax import lax
from jax.experimental import pallas as pl
from jax.experimental.pallas import tpu as pltpu

F32 = jnp.float32
BF16 = jnp.bfloat16

N_HEADS = 8
HEAD_DIM = 64
V_DIM = 2 * HEAD_DIM
N_BUCKETS = 32
MAX_DISTANCE = 128
CONV_K = 3
N_GROUPS = 4
EXPERTS_PER_GROUP = 8
N_EXPERTS = N_GROUPS * EXPERTS_PER_GROUP
TOP_K = 2
LN_EPS = 1e-5
MASK_VALUE = -1e30
LOG2_E = math.log2(math.e)

IN_Q, IN_K, IN_V, IN_CB, IN_CC, IN_CX, IN_GA, IN_GC = range(8)
COL_Q, COL_K, COL_CB, COL_CC, COL_CX, COL_GA, COL_GC = range(7)

LANES = 128
BF16_SUBLANES = 16
ROUTER_LANES = LANES

LN_ROWS = 512
PROJ_TM = 1024
PROJ_TN = 1024
ATTN_T = 512
EXPERT_ROWS = 512
VMEM_LIMIT = 56 * 1024 * 1024


def _cparams(n_axes):
    return pltpu.CompilerParams(dimension_semantics=("arbitrary",) * n_axes,
                                vmem_limit_bytes=VMEM_LIMIT)


def _layer_norm(x, g, b):
    mu = jnp.mean(x, axis=-1, keepdims=True)
    xc = x - mu
    var = jnp.mean(xc * xc, axis=-1, keepdims=True)
    return xc * lax.rsqrt(var + LN_EPS) * g + b


def _ln_kernel(x_ref, g_ref, b_ref, h_ref, hb_ref):
    h = _layer_norm(x_ref[...], g_ref[...], b_ref[...])
    h_ref[...] = h
    hb_ref[...] = h.astype(BF16)


def _input_layer_norm(x, g, b):
    n, d = x.shape
    row = pl.BlockSpec((LN_ROWS, d), lambda i: (i, 0))
    vec = pl.BlockSpec((1, d), lambda i: (0, 0))
    return pl.pallas_call(
        _ln_kernel,
        grid=(n // LN_ROWS,),
        in_specs=[row, vec, vec],
        out_specs=[row, row],
        out_shape=[jax.ShapeDtypeStruct((n, d), F32), jax.ShapeDtypeStruct((n, d), BF16)],
        compiler_params=_cparams(1),
        name="input_layer_norm",
    )(x, g.reshape(1, d), b.reshape(1, d))


def _matmul_kernel(x_ref, w_ref, o_ref):
    o_ref[...] = jnp.dot(x_ref[...], w_ref[...], preferred_element_type=F32).astype(o_ref.dtype)


def _in_projection(hb, w):
    n, d = hb.shape
    c = w.shape[1]
    return pl.pallas_call(
        _matmul_kernel,
        grid=(n // PROJ_TM, c // PROJ_TN),
        in_specs=[pl.BlockSpec((PROJ_TM, d), lambda i, j: (i, 0)),
                  pl.BlockSpec((d, PROJ_TN), lambda i, j: (0, j))],
        out_specs=pl.BlockSpec((PROJ_TM, PROJ_TN), lambda i, j: (i, j)),
        out_shape=jax.ShapeDtypeStruct((n, c), BF16),
        compiler_params=_cparams(2),
        name="in_projection",
    )(hb, w)


def _matmul_nt_kernel(w_ref, x_ref, o_ref):
    o_ref[...] = lax.dot_general(w_ref[...], x_ref[...], (((1,), (1,)), ((), ())),
                                 preferred_element_type=F32).astype(o_ref.dtype)


def _v_projection_t(hb, wvt):
    n, d = hb.shape
    c = wvt.shape[0]
    t = ATTN_T
    return pl.pallas_call(
        _matmul_nt_kernel,
        grid=(n // t,),
        in_specs=[pl.BlockSpec((c, d), lambda i: (0, 0)),
                  pl.BlockSpec((t, d), lambda i: (i, 0))],
        out_specs=pl.BlockSpec((None, c, t), lambda i: (i, 0, 0)),
        out_shape=jax.ShapeDtypeStruct((n // t, c, t), BF16),
        compiler_params=_cparams(1),
        name="v_projection_t",
    )(wvt, hb)


def _rel_bucket(rel):
    n = jnp.maximum(rel, 0)
    max_exact = N_BUCKETS // 2
    nf = jnp.maximum(n, 1).astype(F32)
    large = max_exact + (jnp.log(nf / max_exact) / math.log(MAX_DISTANCE / max_exact)
                         * (N_BUCKETS - max_exact)).astype(jnp.int32)
    large = jnp.minimum(large, N_BUCKETS - 1)
    return jnp.where(n < max_exact, n, large)


def _bias_tiles(rel_table, t):
    assert t >= MAX_DISTANCE
    rel = jnp.arange(t, dtype=jnp.int32)[None, :] - jnp.arange(t, dtype=jnp.int32)[:, None]
    table = rel_table.astype(F32) * LOG2_E

    def lookup(r):
        onehot = (_rel_bucket(r)[:, :, None] == jnp.arange(N_BUCKETS, dtype=jnp.int32)).astype(F32)
        return jnp.einsum('kqb,bh->hkq', onehot, table, precision=lax.Precision.HIGHEST)

    diag = jnp.where((rel >= 0)[None], lookup(rel), MASK_VALUE)
    prev = lookup(rel + t)
    return jnp.stack([diag, prev], axis=1), table[N_BUCKETS - 1]


def _attn_kernel(lam_ref, far_ref, q_ref, k_ref, vt_ref, bias_ref, g_ref, o_ref,
                 m_ref, acc_ref, sa_ref, sb_ref, *, t, post_scale):
    head = pl.program_id(0)
    qi = pl.program_id(2)
    q = (q_ref[...].astype(F32) * (HEAD_DIM ** -0.5 * LOG2_E)).astype(BF16)
    lane = lax.broadcasted_iota(jnp.int32, q.shape, 1)
    zero = jnp.zeros_like(q)
    q_halves = (jnp.where(lane < HEAD_DIM, q, zero), jnp.where(lane >= HEAD_DIM, q, zero))
    ones = jnp.ones((BF16_SUBLANES, t), BF16)

    m_ref[...] = jnp.full(m_ref.shape, MASK_VALUE, F32)
    acc_ref[...] = jnp.zeros(acc_ref.shape, F32)

    def scores(j, s_ref):
        k = k_ref[pl.ds(pl.multiple_of(j * t, t), t), :]
        for br in range(2):
            s_ref[br] = lax.dot_general(k, q_halves[br], (((1,), (1,)), ((), ())),
                                        preferred_element_type=F32)

    def softmax_pv(j, s_ref, bias, shift):
        v_aug = jnp.concatenate([vt_ref[j], ones], axis=0)
        for br in range(2):
            s = s_ref[br]
            if bias is not None:
                s = s + bias
            m_prev = m_ref[br]
            m_tile = jnp.max(s, axis=0, keepdims=True)
            if shift is not None:
                m_tile = m_tile + shift
            m_new = jnp.maximum(m_prev, m_tile)
            alpha = jnp.exp2(m_prev - m_new)
            p = jnp.exp2(s - (m_new if shift is None else m_new - shift)).astype(BF16)
            acc_ref[br] = alpha * acc_ref[br] + jnp.dot(v_aug, p, preferred_element_type=F32)
            m_ref[br] = m_new

    far_bias = far_ref[head]

    @pl.when(qi >= 1)
    def _():
        n_far = qi - 1
        odd = n_far % 2

        @pl.when(odd == 1)
        def _():
            scores(0, sb_ref)
            softmax_pv(0, sb_ref, None, far_bias)

        scores(odd, sa_ref)

        def pair(i, carry):
            j = odd + 2 * i
            scores(j + 1, sb_ref)
            softmax_pv(j, sa_ref, None, far_bias)
            scores(j + 2, sa_ref)
            softmax_pv(j + 1, sb_ref, None, far_bias)
            return carry

        lax.fori_loop(0, n_far // 2, pair, 0)

    scores(qi, sb_ref)

    @pl.when(qi >= 1)
    def _():
        softmax_pv(qi - 1, sa_ref, bias_ref[1], None)

    softmax_pv(qi, sb_ref, bias_ref[0], None)

    a1 = acc_ref[0]
    a2 = acc_ref[1]
    o = a1[:V_DIM] / a1[V_DIM:V_DIM + 1] - lam_ref[0] * (a2[:V_DIM] / a2[V_DIM:V_DIM + 1])
    ms = jnp.mean(o * o, axis=0, keepdims=True)
    on = o * lax.rsqrt(ms + LN_EPS) * (g_ref[...] * post_scale)
    o_ref[...] = on.T.astype(o_ref.dtype)


def _diff_attention(z, vt, lam, bias, far_bias, subln_g, batch, post_scale):
    n = z.shape[0]
    seq = n // batch
    t = ATTN_T
    nq = seq // t
    d_attn = N_HEADS * V_DIM
    smem = pl.BlockSpec(memory_space=pltpu.SMEM)
    return pl.pallas_call(
        functools.partial(_attn_kernel, t=t, post_scale=post_scale),
        grid=(N_HEADS, batch, nq),
        in_specs=[
            smem, smem,
            pl.BlockSpec((t, V_DIM), lambda h, b, i: (b * nq + i, COL_Q * N_HEADS + h)),
            pl.BlockSpec((seq, V_DIM), lambda h, b, i: (b, COL_K * N_HEADS + h)),
            pl.BlockSpec((nq, V_DIM, t), lambda h, b, i: (b, h, 0)),
            pl.BlockSpec((None, 2, t, t), lambda h, b, i: (h, 0, 0, 0)),
            pl.BlockSpec((V_DIM, 1), lambda h, b, i: (0, 0)),
        ],
        out_specs=pl.BlockSpec((t, V_DIM), lambda h, b, i: (b * nq + i, h)),
        out_shape=jax.ShapeDtypeStruct((n, d_attn), BF16),
        scratch_shapes=[pltpu.VMEM((2, 1, t), F32),
                        pltpu.VMEM((2, V_DIM + BF16_SUBLANES, t), F32),
                        pltpu.VMEM((2, t, t), F32), pltpu.VMEM((2, t, t), F32)],
        compiler_params=_cparams(3),
        name="diff_attention",
    )(lam.reshape(1), far_bias, z, z, vt, bias, subln_g.reshape(V_DIM, 1))


def _mixer_tail_kernel(on_ref, cb_ref, cc_ref, cx_ref, ga_ref, gc_ref, cch_ref, cxh_ref, h_ref,
                       wa_ref, wc_ref, wo_ref, convw_ref, gb_ref, lg_ref, lb_ref, wr_ref, br_ref,
                       h1_ref, h1b_ref, logit_ref, *, seq, alpha):
    i = pl.program_id(0)
    tm = on_ref.shape[0]
    attn = jnp.dot(on_ref[...], wa_ref[...], preferred_element_type=F32)

    u = cc_ref[...].astype(F32) * cx_ref[...].astype(F32)
    halo = cch_ref[...].astype(F32) * cxh_ref[...].astype(F32)
    halo = jnp.where((i * tm) % seq == 0, 0.0, halo)
    hm1 = halo[BF16_SUBLANES - 1:BF16_SUBLANES]
    hm2 = halo[BF16_SUBLANES - 2:BF16_SUBLANES - 1]
    row = lax.broadcasted_iota(jnp.int32, u.shape, 0)
    u1 = jnp.where(row == 0, hm1, pltpu.roll(u, 1, axis=0))
    u2 = jnp.where(row == 0, hm2, jnp.where(row == 1, hm1, pltpu.roll(u, 2, axis=0)))
    w = convw_ref[...]
    y = w[2:3] * u + w[1:2] * u1 + w[0:1] * u2
    conv = jnp.dot((cb_ref[...].astype(F32) * y).astype(BF16), wc_ref[...],
                   preferred_element_type=F32)

    gb = gb_ref[...]
    merged = (jax.nn.sigmoid(ga_ref[...].astype(F32) + gb[0:1]) * attn
              + jax.nn.sigmoid(gc_ref[...].astype(F32) + gb[1:2]) * conv)
    mix = jnp.dot(merged.astype(BF16), wo_ref[...], preferred_element_type=F32)
    h1 = _layer_norm(alpha * h_ref[...] + mix, lg_ref[...], lb_ref[...])
    h1b = h1.astype(BF16)
    h1_ref[...] = h1
    h1b_ref[...] = h1b
    logit_ref[...] = jnp.dot(h1b, wr_ref[...], preferred_element_type=F32) + br_ref[...]


def _mixer_tail(on, z, h, wa, wc, wo, conv_w, gate_b, ln_g, ln_b, wr, br, seq, alpha):
    n, d = h.shape
    tm = LN_ROWS
    halo_blocks = tm // BF16_SUBLANES

    def col(c):
        return pl.BlockSpec((tm, d), lambda i: (i, c))

    def halo(c):
        return pl.BlockSpec((BF16_SUBLANES, d), lambda i: (jnp.maximum(i * halo_blocks - 1, 0), c))

    def full(shape):
        return pl.BlockSpec(shape, lambda i: (0,) * len(shape))

    row = pl.BlockSpec((tm, d), lambda i: (i, 0))
    return pl.pallas_call(
        functools.partial(_mixer_tail_kernel, seq=seq, alpha=alpha),
        grid=(n // tm,),
        in_specs=[row, col(COL_CB), col(COL_CC), col(COL_CX), col(COL_GA), col(COL_GC),
                  halo(COL_CC), halo(COL_CX), row,
                  full((d, d)), full((d, d)), full((d, d)), full((CONV_K, d)), full((2, d)),
                  full((1, d)), full((1, d)), full((d, ROUTER_LANES)), full((1, ROUTER_LANES))],
        out_specs=[row, row, pl.BlockSpec((tm, ROUTER_LANES), lambda i: (i, 0))],
        out_shape=[jax.ShapeDtypeStruct((n, d), F32), jax.ShapeDtypeStruct((n, d), BF16),
                   jax.ShapeDtypeStruct((n, ROUTER_LANES), F32)],
        compiler_params=_cparams(1),
        name="mixer_tail",
    )(on, z, z, z, z, z, z, z, h, wa, wc, wo, conv_w, gate_b,
      ln_g.reshape(1, d), ln_b.reshape(1, d), wr, br)


def _route(logits, n_blk):
    n_tok = logits.shape[0]
    n_asg = n_tok * TOP_K
    gl = logits[:, :N_GROUPS]
    g_max = jnp.max(gl, axis=-1, keepdims=True)
    g_idx = jnp.argmax(gl, axis=-1)[:, None].astype(jnp.int32)
    g_p = 1.0 / jnp.sum(jnp.exp(gl - g_max), axis=-1, keepdims=True)
    e_all = logits[:, N_GROUPS:N_GROUPS + N_EXPERTS].reshape(n_tok, N_GROUPS, EXPERTS_PER_GROUP)
    in_group = g_idx[:, :, None] == jnp.arange(N_GROUPS, dtype=jnp.int32)[None, :, None]
    e_logits = jnp.sum(jnp.where(in_group, e_all, 0.0), axis=1)
    loc = jnp.arange(EXPERTS_PER_GROUP, dtype=jnp.int32)[None, :]
    i0 = jnp.argmax(e_logits, axis=-1)[:, None].astype(jnp.int32)
    e0 = jnp.max(e_logits, axis=-1, keepdims=True)
    rest = jnp.where(loc == i0, -jnp.inf, e_logits)
    i1 = jnp.argmax(rest, axis=-1)[:, None].astype(jnp.int32)
    e1 = jnp.max(rest, axis=-1, keepdims=True)
    r = jnp.exp(e1 - e0)
    gate = jnp.concatenate([g_p / (1.0 + r), g_p * r / (1.0 + r)], axis=-1)
    e_loc = jnp.concatenate([i0, i1], axis=-1)
    expert = (g_idx * EXPERTS_PER_GROUP + e_loc).reshape(-1).astype(jnp.int32)

    onehot = (expert[:, None] == jnp.arange(N_EXPERTS, dtype=jnp.int32)[None, :]).astype(jnp.int32)
    chunk = EXPERT_ROWS
    tri = jnp.tril(jnp.ones((chunk, chunk), BF16))
    local = jnp.einsum('ij,cjk->cik', tri, onehot.astype(BF16).reshape(n_asg // chunk, chunk, N_EXPERTS),
                       preferred_element_type=F32).astype(jnp.int32)
    totals = local[:, -1, :]
    csum = (local + (jnp.cumsum(totals, axis=0) - totals)[:, None, :]).reshape(n_asg, N_EXPERTS)
    sizes = csum[-1]
    padded = (sizes + EXPERT_ROWS - 1) // EXPERT_ROWS * EXPERT_ROWS
    pad_end = jnp.cumsum(padded)
    pad_start = pad_end - padded
    dest = jnp.sum(onehot * (csum - 1 + pad_start[None, :]), axis=1)
    src_tok = (jnp.arange(n_blk * EXPERT_ROWS, dtype=jnp.int32) % n_tok).at[dest].set(
        jnp.arange(n_asg, dtype=jnp.int32) // TOP_K)
    blk_e = jnp.minimum(jnp.searchsorted(pad_end, jnp.arange(n_blk, dtype=jnp.int32) * EXPERT_ROWS,
                                         side='right'), N_EXPERTS - 1).astype(jnp.int32)
    n_used = (pad_end[-1] // EXPERT_ROWS).astype(jnp.int32).reshape(1)
    return gate, dest.reshape(n_tok, TOP_K), src_tok, blk_e, n_used


def _expert_kernel(blk_e_ref, n_used_ref, x_ref, w1_ref, w3_ref, w2_ref, y_ref,
                   w1b_ref, w3b_ref, w2b_ref):
    i = pl.program_id(0)
    new_expert = jnp.logical_or(i == 0, blk_e_ref[i] != blk_e_ref[jnp.maximum(i - 1, 0)])

    @pl.when(jnp.logical_and(i < n_used_ref[0], new_expert))
    def _():
        w1b_ref[...] = w1_ref[...].astype(BF16)
        w3b_ref[...] = w3_ref[...].astype(BF16)
        w2b_ref[...] = w2_ref[...].astype(BF16)

    @pl.when(i < n_used_ref[0])
    def _():
        x = x_ref[...]
        a = jnp.dot(x, w1b_ref[...], preferred_element_type=F32)
        b = jnp.dot(x, w3b_ref[...], preferred_element_type=F32)
        mid = (a * jax.nn.sigmoid(a) * b).astype(BF16)
        y_ref[...] = jnp.dot(mid, w2b_ref[...], preferred_element_type=F32).astype(y_ref.dtype)

    @pl.when(i >= n_used_ref[0])
    def _():
        y_ref[...] = jnp.zeros(y_ref.shape, y_ref.dtype)


def _expert_mlp(xbuf, blk_e, n_used, w1, w3, w2, layer):
    n_rows, d = xbuf.shape
    de = w1.shape[-1]
    n_blk = n_rows // EXPERT_ROWS
    grid_spec = pltpu.PrefetchScalarGridSpec(
        num_scalar_prefetch=2,
        grid=(n_blk,),
        in_specs=[
            pl.BlockSpec((EXPERT_ROWS, d), lambda i, be, nu: (jnp.minimum(i, nu[0] - 1), 0)),
            pl.BlockSpec((None, None, d, de), lambda i, be, nu: (layer, be[i], 0, 0)),
            pl.BlockSpec((None, None, d, de), lambda i, be, nu: (layer, be[i], 0, 0)),
            pl.BlockSpec((None, None, de, d), lambda i, be, nu: (layer, be[i], 0, 0)),
        ],
        out_specs=pl.BlockSpec((EXPERT_ROWS, d), lambda i, be, nu: (i, 0)),
        scratch_shapes=[pltpu.VMEM((d, de), BF16), pltpu.VMEM((d, de), BF16),
                        pltpu.VMEM((de, d), BF16)],
    )
    return pl.pallas_call(
        _expert_kernel,
        grid_spec=grid_spec,
        out_shape=jax.ShapeDtypeStruct((n_rows, d), BF16),
        compiler_params=_cparams(1),
        name="expert_mlp",
    )(blk_e, n_used, xbuf, w1, w3, w2)


def _combine_kernel(h_ref, y0_ref, y1_ref, gate_ref, g_ref, b_ref, o_ref, ob_ref, *, alpha):
    gate = gate_ref[...]
    ffn = gate[:, 0:1] * y0_ref[...].astype(F32) + gate[:, 1:2] * y1_ref[...].astype(F32)
    h2 = _layer_norm(alpha * h_ref[...] + ffn, g_ref[...], b_ref[...])
    o_ref[...] = h2
    ob_ref[...] = h2.astype(BF16)


def _combine(h, y_both, gate, ln_g, ln_b, alpha):
    n, d = h.shape
    n_tiles = n // LN_ROWS
    row = pl.BlockSpec((LN_ROWS, d), lambda i: (i, 0))
    vec = pl.BlockSpec((1, d), lambda i: (0, 0))
    return pl.pallas_call(
        functools.partial(_combine_kernel, alpha=alpha),
        grid=(n // LN_ROWS,),
        in_specs=[row, row, pl.BlockSpec((LN_ROWS, d), lambda i: (i + n_tiles, 0)),
                  pl.BlockSpec((LN_ROWS, TOP_K), lambda i: (i, 0)), vec, vec],
        out_specs=[row, row],
        out_shape=[jax.ShapeDtypeStruct((n, d), F32), jax.ShapeDtypeStruct((n, d), BF16)],
        compiler_params=_cparams(1),
        name="moe_combine",
    )(h, y_both, y_both, gate, ln_g.reshape(1, d), ln_b.reshape(1, d))


def kernel(x, ln0_g, ln0_b, rel_table, w_in, gate_b, lam_vecs, subln_g, w_attn_proj, conv_w,
           w_conv_proj, w_out, ln1_g, ln1_b, w_rg, b_rg, w_re, b_re, w1, w3, w2, ln2_g, ln2_b):
    batch, seq, d = x.shape
    depth = w_in.shape[0]
    n_tok = batch * seq
    alpha = (2 * depth) ** 0.25
    assert seq % ATTN_T == 0 and n_tok % PROJ_TM == 0 and seq % LN_ROWS == 0
    n_asg = n_tok * TOP_K
    n_blk = -(-(n_asg + N_EXPERTS * (EXPERT_ROWS - 1)) // EXPERT_ROWS)

    bias, far_bias = _bias_tiles(rel_table, ATTN_T)
    h, hb = _input_layer_norm(x.reshape(n_tok, d), ln0_g, ln0_b)
    for l in range(depth):
        lam_init = 0.8 - 0.6 * math.exp(-0.3 * l)
        lv = lam_vecs[l].astype(F32)
        lam = jnp.exp(jnp.sum(lv[0] * lv[1])) - jnp.exp(jnp.sum(lv[2] * lv[3])) + lam_init

        w_l = w_in[l]
        w_tok = jnp.concatenate([w_l[:, :IN_V * d], w_l[:, IN_CB * d:]], axis=1).astype(BF16)
        w_vt = w_l[:, IN_V * d:IN_CB * d].T.astype(BF16)
        z = _in_projection(hb, w_tok)
        vt = _v_projection_t(hb, w_vt)
        on = _diff_attention(z, vt, lam, bias, far_bias, subln_g[l], batch, 1.0 - lam_init)
        w_router = jnp.zeros((d, ROUTER_LANES), F32)
        w_router = w_router.at[:, :N_GROUPS].set(w_rg[l]).at[:, N_GROUPS:N_GROUPS + N_EXPERTS].set(w_re[l])
        b_router = jnp.zeros((1, ROUTER_LANES), F32)
        b_router = b_router.at[0, :N_GROUPS].set(b_rg[l]).at[0, N_GROUPS:N_GROUPS + N_EXPERTS].set(b_re[l])
        h1, h1b, logits = _mixer_tail(
            on, z, h, w_attn_proj[l].astype(BF16), w_conv_proj[l].astype(BF16), w_out[l].astype(BF16),
            conv_w[l], gate_b[l], ln1_g[l], ln1_b[l], w_router.astype(BF16), b_router, seq, alpha)

        gate, dest, src_tok, blk_e, n_used = _route(logits, n_blk)
        xbuf = jnp.take(h1b, src_tok, axis=0, mode="clip")
        ybuf = _expert_mlp(xbuf, blk_e, n_used, w1, w3, w2, l)
        y_both = jnp.take(ybuf, jnp.concatenate([dest[:, 0], dest[:, 1]]), axis=0,
                          mode="clip")
        h, hb = _combine(h1, y_both, gate, ln2_g[l], ln2_b[l], alpha)
    return h.reshape(batch, seq, d)
```

```python
import functools
import math

import jax
import jax.numpy as jnp
from jax import lax
from jax.experimental import pallas as pl
from jax.experimental.pallas import tpu as pltpu

F32 = jnp.float32
BF16 = jnp.bfloat16

N_HEADS = 8
HEAD_DIM = 64
V_DIM = 2 * HEAD_DIM
N_BUCKETS = 32
MAX_DISTANCE = 128
CONV_K = 3
N_GROUPS = 4
EXPERTS_PER_GROUP = 8
N_EXPERTS = N_GROUPS * EXPERTS_PER_GROUP
TOP_K = 2
LN_EPS = 1e-5
MASK_VALUE = -1e30
LOG2_E = math.log2(math.e)

IN_Q, IN_K, IN_V, IN_CB, IN_CC, IN_CX, IN_GA, IN_GC = range(8)
COL_Q, COL_K, COL_CB, COL_CC, COL_CX, COL_GA, COL_GC = range(7)

LANES = 128
BF16_SUBLANES = 16
ROUTER_LANES = LANES

LN_ROWS = 512
PROJ_TM = 1024
PROJ_TN = 1024
ATTN_T = 1024
EXPERT_ROWS = 512
VMEM_LIMIT = 56 * 1024 * 1024


def _cparams(n_axes):
    return pltpu.CompilerParams(dimension_semantics=("arbitrary",) * n_axes,
                                vmem_limit_bytes=VMEM_LIMIT)


def _layer_norm(x, g, b):
    mu = jnp.mean(x, axis=-1, keepdims=True)
    xc = x - mu
    var = jnp.mean(xc * xc, axis=-1, keepdims=True)
    return xc * lax.rsqrt(var + LN_EPS) * g + b


def _ln_kernel(x_ref, g_ref, b_ref, h_ref, hb_ref):
    h = _layer_norm(x_ref[...], g_ref[...], b_ref[...])
    h_ref[...] = h
    hb_ref[...] = h.astype(BF16)


def _input_layer_norm(x, g, b):
    n, d = x.shape
    row = pl.BlockSpec((LN_ROWS, d), lambda i: (i, 0))
    vec = pl.BlockSpec((1, d), lambda i: (0, 0))
    return pl.pallas_call(
        _ln_kernel,
        grid=(n // LN_ROWS,),
        in_specs=[row, vec, vec],
        out_specs=[row, row],
        out_shape=[jax.ShapeDtypeStruct((n, d), F32), jax.ShapeDtypeStruct((n, d), BF16)],
        compiler_params=_cparams(1),
        name="input_layer_norm",
    )(x, g.reshape(1, d), b.reshape(1, d))


def _matmul_kernel(x_ref, w_ref, o_ref):
    o_ref[...] = jnp.dot(x_ref[...], w_ref[...], preferred_element_type=F32).astype(o_ref.dtype)


def _in_projection(hb, w):
    n, d = hb.shape
    c = w.shape[1]
    return pl.pallas_call(
        _matmul_kernel,
        grid=(n // PROJ_TM, c // PROJ_TN),
        in_specs=[pl.BlockSpec((PROJ_TM, d), lambda i, j: (i, 0)),
                  pl.BlockSpec((d, PROJ_TN), lambda i, j: (0, j))],
        out_specs=pl.BlockSpec((PROJ_TM, PROJ_TN), lambda i, j: (i, j)),
        out_shape=jax.ShapeDtypeStruct((n, c), BF16),
        compiler_params=_cparams(2),
        name="in_projection",
    )(hb, w)


def _matmul_nt_kernel(w_ref, x_ref, o_ref):
    o_ref[...] = lax.dot_general(w_ref[...], x_ref[...], (((1,), (1,)), ((), ())),
                                 preferred_element_type=F32).astype(o_ref.dtype)


def _v_projection_t(hb, wvt):
    n, d = hb.shape
    c = wvt.shape[0]
    t = ATTN_T
    return pl.pallas_call(
        _matmul_nt_kernel,
        grid=(n // t,),
        in_specs=[pl.BlockSpec((c, d), lambda i: (0, 0)),
                  pl.BlockSpec((t, d), lambda i: (i, 0))],
        out_specs=pl.BlockSpec((None, c, t), lambda i: (i, 0, 0)),
        out_shape=jax.ShapeDtypeStruct((n // t, c, t), BF16),
        compiler_params=_cparams(1),
        name="v_projection_t",
    )(wvt, hb)


def _rel_bucket(rel):
    n = jnp.maximum(rel, 0)
    max_exact = N_BUCKETS // 2
    nf = jnp.maximum(n, 1).astype(F32)
    large = max_exact + (jnp.log(nf / max_exact) / math.log(MAX_DISTANCE / max_exact)
                         * (N_BUCKETS - max_exact)).astype(jnp.int32)
    large = jnp.minimum(large, N_BUCKETS - 1)
    return jnp.where(n < max_exact, n, large)


def _bias_tiles(rel_table, t):
    assert t >= MAX_DISTANCE
    rel = jnp.arange(t, dtype=jnp.int32)[None, :] - jnp.arange(t, dtype=jnp.int32)[:, None]
    table = rel_table.astype(F32) * LOG2_E

    def lookup(r):
        onehot = (_rel_bucket(r)[:, :, None] == jnp.arange(N_BUCKETS, dtype=jnp.int32)).astype(F32)
        return jnp.einsum('kqb,bh->hkq', onehot, table, precision=lax.Precision.HIGHEST)

    diag = jnp.where((rel >= 0)[None], lookup(rel), MASK_VALUE)
    prev = lookup(rel + t)
    return jnp.stack([diag, prev], axis=1), table[N_BUCKETS - 1]


def _attn_kernel(lam_ref, far_ref, q_ref, k_ref, vt_ref, bias_ref, g_ref, o_ref,
                 m_ref, acc_ref, sa_ref, sb_ref, *, t, post_scale):
    head = pl.program_id(0)
    qi = pl.program_id(2)
    q = (q_ref[...].astype(F32) * (HEAD_DIM ** -0.5 * LOG2_E)).astype(BF16)
    lane = lax.broadcasted_iota(jnp.int32, q.shape, 1)
    zero = jnp.zeros_like(q)
    q_halves = (jnp.where(lane < HEAD_DIM, q, zero), jnp.where(lane >= HEAD_DIM, q, zero))
    ones = jnp.ones((BF16_SUBLANES, t), BF16)

    m_ref[...] = jnp.full(m_ref.shape, MASK_VALUE, F32)
    acc_ref[...] = jnp.zeros(acc_ref.shape, F32)

    def scores(j, s_ref):
        k = k_ref[pl.ds(pl.multiple_of(j * t, t), t), :]
        for br in range(2):
            s_ref[br] = lax.dot_general(k, q_halves[br], (((1,), (1,)), ((), ())),
                                        preferred_element_type=F32)

    def softmax_pv(j, s_ref, bias, shift):
        v_aug = jnp.concatenate([vt_ref[j], ones], axis=0)
        for br in range(2):
            s = s_ref[br]
            if bias is not None:
                s = s + bias
            m_prev = m_ref[br]
            m_tile = jnp.max(s, axis=0, keepdims=True)
            if shift is not None:
                m_tile = m_tile + shift
            m_new = jnp.maximum(m_prev, m_tile)
            alpha = jnp.exp2(m_prev - m_new)
            p = jnp.exp2(s - (m_new if shift is None else m_new - shift)).astype(BF16)
            acc_ref[br] = alpha * acc_ref[br] + jnp.dot(v_aug, p, preferred_element_type=F32)
            m_ref[br] = m_new

    far_bias = far_ref[head]

    @pl.when(qi >= 1)
    def _():
        n_far = qi - 1
        odd = n_far % 2

        @pl.when(odd == 1)
        def _():
            scores(0, sb_ref)
            softmax_pv(0, sb_ref, None, far_bias)

        scores(odd, sa_ref)

        def pair(i, carry):
            j = odd + 2 * i
            scores(j + 1, sb_ref)
            softmax_pv(j, sa_ref, None, far_bias)
            scores(j + 2, sa_ref)
            softmax_pv(j + 1, sb_ref, None, far_bias)
            return carry

        lax.fori_loop(0, n_far // 2, pair, 0)

    scores(qi, sb_ref)

    @pl.when(qi >= 1)
    def _():
        softmax_pv(qi - 1, sa_ref, bias_ref[1], None)

    softmax_pv(qi, sb_ref, bias_ref[0], None)

    a1 = acc_ref[0]
    a2 = acc_ref[1]
    o = a1[:V_DIM] / a1[V_DIM:V_DIM + 1] - lam_ref[0] * (a2[:V_DIM] / a2[V_DIM:V_DIM + 1])
    ms = jnp.mean(o * o, axis=0, keepdims=True)
    on = o * lax.rsqrt(ms + LN_EPS) * (g_ref[...] * post_scale)
    o_ref[...] = on.T.astype(o_ref.dtype)


def _diff_attention(z, vt, lam, bias, far_bias, subln_g, batch, post_scale):
    n = z.shape[0]
    seq = n // batch
    t = ATTN_T
    nq = seq // t
    d_attn = N_HEADS * V_DIM
    smem = pl.BlockSpec(memory_space=pltpu.SMEM)
    once = pl.Buffered(1)
    return pl.pallas_call(
        functools.partial(_attn_kernel, t=t, post_scale=post_scale),
        grid=(N_HEADS, batch, nq),
        in_specs=[
            smem, smem,
            pl.BlockSpec((t, V_DIM), lambda h, b, i: (b * nq + i, COL_Q * N_HEADS + h)),
            pl.BlockSpec((seq, V_DIM), lambda h, b, i: (b, COL_K * N_HEADS + h), pipeline_mode=once),
            pl.BlockSpec((nq, V_DIM, t), lambda h, b, i: (b, h, 0), pipeline_mode=once),
            pl.BlockSpec((None, 2, t, t), lambda h, b, i: (h, 0, 0, 0), pipeline_mode=once),
            pl.BlockSpec((V_DIM, 1), lambda h, b, i: (0, 0)),
        ],
        out_specs=pl.BlockSpec((t, V_DIM), lambda h, b, i: (b * nq + i, h)),
        out_shape=jax.ShapeDtypeStruct((n, d_attn), BF16),
        scratch_shapes=[pltpu.VMEM((2, 1, t), F32),
                        pltpu.VMEM((2, V_DIM + BF16_SUBLANES, t), F32),
                        pltpu.VMEM((2, t, t), F32), pltpu.VMEM((2, t, t), F32)],
        compiler_params=_cparams(3),
        name="diff_attention",
    )(lam.reshape(1), far_bias, z, z, vt, bias, subln_g.reshape(V_DIM, 1))


def _mixer_tail_kernel(on_ref, cb_ref, cc_ref, cx_ref, ga_ref, gc_ref, cch_ref, cxh_ref, h_ref,
                       wa_ref, wc_ref, wo_ref, convw_ref, gb_ref, lg_ref, lb_ref, wr_ref, br_ref,
                       h1_ref, h1b_ref, logit_ref, *, seq, alpha):
    i = pl.program_id(0)
    tm = on_ref.shape[0]
    attn = jnp.dot(on_ref[...], wa_ref[...], preferred_element_type=F32)

    u = cc_ref[...].astype(F32) * cx_ref[...].astype(F32)
    halo = cch_ref[...].astype(F32) * cxh_ref[...].astype(F32)
    halo = jnp.where((i * tm) % seq == 0, 0.0, halo)
    hm1 = halo[BF16_SUBLANES - 1:BF16_SUBLANES]
    hm2 = halo[BF16_SUBLANES - 2:BF16_SUBLANES - 1]
    row = lax.broadcasted_iota(jnp.int32, u.shape, 0)
    u1 = jnp.where(row == 0, hm1, pltpu.roll(u, 1, axis=0))
    u2 = jnp.where(row == 0, hm2, jnp.where(row == 1, hm1, pltpu.roll(u, 2, axis=0)))
    w = convw_ref[...]
    y = w[2:3] * u + w[1:2] * u1 + w[0:1] * u2
    conv = jnp.dot((cb_ref[...].astype(F32) * y).astype(BF16), wc_ref[...],
                   preferred_element_type=F32)

    gb = gb_ref[...]
    merged = (jax.nn.sigmoid(ga_ref[...].astype(F32) + gb[0:1]) * attn
              + jax.nn.sigmoid(gc_ref[...].astype(F32) + gb[1:2]) * conv)
    mix = jnp.dot(merged.astype(BF16), wo_ref[...], preferred_element_type=F32)
    h1 = _layer_norm(alpha * h_ref[...] + mix, lg_ref[...], lb_ref[...])
    h1b = h1.astype(BF16)
    h1_ref[...] = h1
    h1b_ref[...] = h1b
    logit_ref[...] = jnp.dot(h1b, wr_ref[...], preferred_element_type=F32) + br_ref[...]


def _mixer_tail(on, z, h, wa, wc, wo, conv_w, gate_b, ln_g, ln_b, wr, br, seq, alpha):
    n, d = h.shape
    tm = LN_ROWS
    halo_blocks = tm // BF16_SUBLANES

    def col(c):
        return pl.BlockSpec((tm, d), lambda i: (i, c))

    def halo(c):
        return pl.BlockSpec((BF16_SUBLANES, d), lambda i: (jnp.maximum(i * halo_blocks - 1, 0), c))

    def full(shape):
        return pl.BlockSpec(shape, lambda i: (0,) * len(shape))

    row = pl.BlockSpec((tm, d), lambda i: (i, 0))
    return pl.pallas_call(
        functools.partial(_mixer_tail_kernel, seq=seq, alpha=alpha),
        grid=(n // tm,),
        in_specs=[row, col(COL_CB), col(COL_CC), col(COL_CX), col(COL_GA), col(COL_GC),
                  halo(COL_CC), halo(COL_CX), row,
                  full((d, d)), full((d, d)), full((d, d)), full((CONV_K, d)), full((2, d)),
                  full((1, d)), full((1, d)), full((d, ROUTER_LANES)), full((1, ROUTER_LANES))],
        out_specs=[row, row, pl.BlockSpec((tm, ROUTER_LANES), lambda i: (i, 0))],
        out_shape=[jax.ShapeDtypeStruct((n, d), F32), jax.ShapeDtypeStruct((n, d), BF16),
                   jax.ShapeDtypeStruct((n, ROUTER_LANES), F32)],
        compiler_params=_cparams(1),
        name="mixer_tail",
    )(on, z, z, z, z, z, z, z, h, wa, wc, wo, conv_w, gate_b,
      ln_g.reshape(1, d), ln_b.reshape(1, d), wr, br)


def _route(logits, n_blk):
    n_tok = logits.shape[0]
    n_asg = n_tok * TOP_K
    gl = logits[:, :N_GROUPS]
    g_max = jnp.max(gl, axis=-1, keepdims=True)
    g_idx = jnp.argmax(gl, axis=-1)[:, None].astype(jnp.int32)
    g_p = 1.0 / jnp.sum(jnp.exp(gl - g_max), axis=-1, keepdims=True)
    e_all = logits[:, N_GROUPS:N_GROUPS + N_EXPERTS].reshape(n_tok, N_GROUPS, EXPERTS_PER_GROUP)
    in_group = g_idx[:, :, None] == jnp.arange(N_GROUPS, dtype=jnp.int32)[None, :, None]
    e_logits = jnp.sum(jnp.where(in_group, e_all, 0.0), axis=1)
    loc = jnp.arange(EXPERTS_PER_GROUP, dtype=jnp.int32)[None, :]
    i0 = jnp.argmax(e_logits, axis=-1)[:, None].astype(jnp.int32)
    e0 = jnp.max(e_logits, axis=-1, keepdims=True)
    rest = jnp.where(loc == i0, -jnp.inf, e_logits)
    i1 = jnp.argmax(rest, axis=-1)[:, None].astype(jnp.int32)
    e1 = jnp.max(rest, axis=-1, keepdims=True)
    r = jnp.exp(e1 - e0)
    gate = jnp.concatenate([g_p / (1.0 + r), g_p * r / (1.0 + r)], axis=-1)
    e_loc = jnp.concatenate([i0, i1], axis=-1)
    expert = (g_idx * EXPERTS_PER_GROUP + e_loc).reshape(-1).astype(jnp.int32)

    onehot = (expert[:, None] == jnp.arange(N_EXPERTS, dtype=jnp.int32)[None, :]).astype(jnp.int32)
    chunk = EXPERT_ROWS
    tri = jnp.tril(jnp.ones((chunk, chunk), BF16))
    local = jnp.einsum('ij,cjk->cik', tri, onehot.astype(BF16).reshape(n_asg // chunk, chunk, N_EXPERTS),
                       preferred_element_type=F32).astype(jnp.int32)
    totals = local[:, -1, :]
    csum = (local + (jnp.cumsum(totals, axis=0) - totals)[:, None, :]).reshape(n_asg, N_EXPERTS)
    sizes = csum[-1]
    padded = (sizes + EXPERT_ROWS - 1) // EXPERT_ROWS * EXPERT_ROWS
    pad_end = jnp.cumsum(padded)
    pad_start = pad_end - padded
    dest = jnp.sum(onehot * (csum - 1 + pad_start[None, :]), axis=1)
    src_tok = (jnp.arange(n_blk * EXPERT_ROWS, dtype=jnp.int32) % n_tok).at[dest].set(
        jnp.arange(n_asg, dtype=jnp.int32) // TOP_K)
    blk_row = jnp.arange(n_blk, dtype=jnp.int32) * EXPERT_ROWS
    blk_e = jnp.minimum(jnp.sum((pad_end[None, :] <= blk_row[:, None]).astype(jnp.int32), axis=1),
                        N_EXPERTS - 1)
    n_used = (pad_end[-1] // EXPERT_ROWS).astype(jnp.int32).reshape(1)
    return gate, dest.reshape(n_tok, TOP_K), src_tok, blk_e, n_used


def _expert_kernel(blk_e_ref, n_used_ref, x_ref, w1_ref, w3_ref, w2_ref, y_ref,
                   w1b_ref, w3b_ref, w2b_ref):
    i = pl.program_id(0)
    new_expert = jnp.logical_or(i == 0, blk_e_ref[i] != blk_e_ref[jnp.maximum(i - 1, 0)])

    @pl.when(jnp.logical_and(i < n_used_ref[0], new_expert))
    def _():
        w1b_ref[...] = w1_ref[...].astype(BF16)
        w3b_ref[...] = w3_ref[...].astype(BF16)
        w2b_ref[...] = w2_ref[...].astype(BF16)

    @pl.when(i < n_used_ref[0])
    def _():
        x = x_ref[...]
        a = jnp.dot(x, w1b_ref[...], preferred_element_type=F32)
        b = jnp.dot(x, w3b_ref[...], preferred_element_type=F32)
        mid = (a * jax.nn.sigmoid(a) * b).astype(BF16)
        y_ref[...] = jnp.dot(mid, w2b_ref[...], preferred_element_type=F32).astype(y_ref.dtype)

    @pl.when(i >= n_used_ref[0])
    def _():
        y_ref[...] = jnp.zeros(y_ref.shape, y_ref.dtype)


def _expert_mlp(xbuf, blk_e, n_used, w1, w3, w2, layer):
    n_rows, d = xbuf.shape
    de = w1.shape[-1]
    n_blk = n_rows // EXPERT_ROWS
    grid_spec = pltpu.PrefetchScalarGridSpec(
        num_scalar_prefetch=2,
        grid=(n_blk,),
        in_specs=[
            pl.BlockSpec((EXPERT_ROWS, d), lambda i, be, nu: (jnp.minimum(i, nu[0] - 1), 0)),
            pl.BlockSpec((None, None, d, de), lambda i, be, nu: (layer, be[i], 0, 0)),
            pl.BlockSpec((None, None, d, de), lambda i, be, nu: (layer, be[i], 0, 0)),
            pl.BlockSpec((None, None, de, d), lambda i, be, nu: (layer, be[i], 0, 0)),
        ],
        out_specs=pl.BlockSpec((EXPERT_ROWS, d), lambda i, be, nu: (i, 0)),
        scratch_shapes=[pltpu.VMEM((d, de), BF16), pltpu.VMEM((d, de), BF16),
                        pltpu.VMEM((de, d), BF16)],
    )
    return pl.pallas_call(
        _expert_kernel,
        grid_spec=grid_spec,
        out_shape=jax.ShapeDtypeStruct((n_rows, d), BF16),
        compiler_params=_cparams(1),
        name="expert_mlp",
    )(blk_e, n_used, xbuf, w1, w3, w2)


def _combine_kernel(h_ref, y0_ref, y1_ref, gate_ref, g_ref, b_ref, o_ref, ob_ref, *, alpha):
    gate = gate_ref[...]
    ffn = gate[:, 0:1] * y0_ref[...].astype(F32) + gate[:, 1:2] * y1_ref[...].astype(F32)
    h2 = _layer_norm(alpha * h_ref[...] + ffn, g_ref[...], b_ref[...])
    o_ref[...] = h2
    ob_ref[...] = h2.astype(BF16)


def _combine(h, y_both, gate, ln_g, ln_b, alpha):
    n, d = h.shape
    n_tiles = n // LN_ROWS
    row = pl.BlockSpec((LN_ROWS, d), lambda i: (i, 0))
    vec = pl.BlockSpec((1, d), lambda i: (0, 0))
    return pl.pallas_call(
        functools.partial(_combine_kernel, alpha=alpha),
        grid=(n // LN_ROWS,),
        in_specs=[row, row, pl.BlockSpec((LN_ROWS, d), lambda i: (i + n_tiles, 0)),
                  pl.BlockSpec((LN_ROWS, TOP_K), lambda i: (i, 0)), vec, vec],
        out_specs=[row, row],
        out_shape=[jax.ShapeDtypeStruct((n, d), F32), jax.ShapeDtypeStruct((n, d), BF16)],
        compiler_params=_cparams(1),
        name="moe_combine",
    )(h, y_both, y_both, gate, ln_g.reshape(1, d), ln_b.reshape(1, d))


def kernel(x, ln0_g, ln0_b, rel_table, w_in, gate_b, lam_vecs, subln_g, w_attn_proj, conv_w,
           w_conv_proj, w_out, ln1_g, ln1_b, w_rg, b_rg, w_re, b_re, w1, w3, w2, ln2_g, ln2_b):
    batch, seq, d = x.shape
    depth = w_in.shape[0]
    n_tok = batch * seq
    alpha = (2 * depth) ** 0.25
    assert seq % ATTN_T == 0 and n_tok % PROJ_TM == 0 and seq % LN_ROWS == 0
    n_asg = n_tok * TOP_K
    n_blk = -(-(n_asg + N_EXPERTS * (EXPERT_ROWS - 1)) // EXPERT_ROWS)

    bias, far_bias = _bias_tiles(rel_table, ATTN_T)
    h, hb = _input_layer_norm(x.reshape(n_tok, d), ln0_g, ln0_b)
    for l in range(depth):
        lam_init = 0.8 - 0.6 * math.exp(-0.3 * l)
        lv = lam_vecs[l].astype(F32)
        lam = jnp.exp(jnp.sum(lv[0] * lv[1])) - jnp.exp(jnp.sum(lv[2] * lv[3])) + lam_init

        w_l = w_in[l]
        w_tok = jnp.concatenate([w_l[:, :IN_V * d], w_l[:, IN_CB * d:]], axis=1).astype(BF16)
        w_vt = w_l[:, IN_V * d:IN_CB * d].T.astype(BF16)
        z = _in_projection(hb, w_tok)
        vt = _v_projection_t(hb, w_vt)
        on = _diff_attention(z, vt, lam, bias, far_bias, subln_g[l], batch, 1.0 - lam_init)
        w_router = jnp.zeros((d, ROUTER_LANES), F32)
        w_router = w_router.at[:, :N_GROUPS].set(w_rg[l]).at[:, N_GROUPS:N_GROUPS + N_EXPERTS].set(w_re[l])
        b_router = jnp.zeros((1, ROUTER_LANES), F32)
        b_router = b_router.at[0, :N_GROUPS].set(b_rg[l]).at[0, N_GROUPS:N_GROUPS + N_EXPERTS].set(b_re[l])
        h1, h1b, logits = _mixer_tail(
            on, z, h, w_attn_proj[l].astype(BF16), w_conv_proj[l].astype(BF16), w_out[l].astype(BF16),
            conv_w[l], gate_b[l], ln1_g[l], ln1_b[l], w_router.astype(BF16), b_router, seq, alpha)

        gate, dest, src_tok, blk_e, n_used = _route(logits, n_blk)
        xbuf = jnp.take(h1b, src_tok, axis=0, mode="clip")
        ybuf = _expert_mlp(xbuf, blk_e, n_used, w1, w3, w2, l)
        y_both = jnp.take(ybuf, jnp.concatenate([dest[:, 0], dest[:, 1]]), axis=0,
                          mode="clip")
        h, hb = _combine(h1, y_both, gate, ln2_g[l], ln2_b[l], alpha)
    return h.reshape(batch, seq, d)
```

```python
import functools
import math

import jax
import jax.numpy as jnp
from jax import lax
from jax.experimental import pallas as pl
from jax.experimental.pallas import tpu as pltpu

F32 = jnp.float32
BF16 = jnp.bfloat16

N_HEADS = 8
HEAD_DIM = 64
V_DIM = 2 * HEAD_DIM
N_BUCKETS = 32
MAX_DISTANCE = 128
CONV_K = 3
N_GROUPS = 4
EXPERTS_PER_GROUP = 8
N_EXPERTS = N_GROUPS * EXPERTS_PER_GROUP
TOP_K = 2
LN_EPS = 1e-5
MASK_VALUE = -1e30
LOG2_E = math.log2(math.e)

IN_Q, IN_K, IN_V, IN_CB, IN_CC, IN_CX, IN_GA, IN_GC = range(8)
COL_Q, COL_K, COL_CB, COL_CC, COL_CX, COL_GA, COL_GC = range(7)

LANES = 128
BF16_SUBLANES = 16
ROUTER_LANES = LANES

LN_ROWS = 512
PROJ_TM = 1024
PROJ_TN = 1024
ATTN_T = 1024
EXPERT_ROWS = 512
VMEM_LIMIT = 56 * 1024 * 1024


def _cparams(n_axes, flags=None):
    return pltpu.CompilerParams(dimension_semantics=("arbitrary",) * n_axes,
                                vmem_limit_bytes=VMEM_LIMIT, flags=flags)


def _layer_norm(x, g, b):
    mu = jnp.mean(x, axis=-1, keepdims=True)
    xc = x - mu
    var = jnp.mean(xc * xc, axis=-1, keepdims=True)
    return xc * lax.rsqrt(var + LN_EPS) * g + b


def _ln_kernel(x_ref, g_ref, b_ref, h_ref, hb_ref):
    h = _layer_norm(x_ref[...], g_ref[...], b_ref[...])
    h_ref[...] = h
    hb_ref[...] = h.astype(BF16)


def _input_layer_norm(x, g, b):
    n, d = x.shape
    row = pl.BlockSpec((LN_ROWS, d), lambda i: (i, 0))
    vec = pl.BlockSpec((1, d), lambda i: (0, 0))
    return pl.pallas_call(
        _ln_kernel,
        grid=(n // LN_ROWS,),
        in_specs=[row, vec, vec],
        out_specs=[row, row],
        out_shape=[jax.ShapeDtypeStruct((n, d), F32), jax.ShapeDtypeStruct((n, d), BF16)],
        compiler_params=_cparams(1),
        name="input_layer_norm",
    )(x, g.reshape(1, d), b.reshape(1, d))


def _matmul_kernel(x_ref, w_ref, o_ref):
    o_ref[...] = jnp.dot(x_ref[...], w_ref[...], preferred_element_type=F32).astype(o_ref.dtype)


def _in_projection(hb, w):
    n, d = hb.shape
    c = w.shape[1]
    return pl.pallas_call(
        _matmul_kernel,
        grid=(n // PROJ_TM, c // PROJ_TN),
        in_specs=[pl.BlockSpec((PROJ_TM, d), lambda i, j: (i, 0)),
                  pl.BlockSpec((d, PROJ_TN), lambda i, j: (0, j))],
        out_specs=pl.BlockSpec((PROJ_TM, PROJ_TN), lambda i, j: (i, j)),
        out_shape=jax.ShapeDtypeStruct((n, c), BF16),
        compiler_params=_cparams(2),
        name="in_projection",
    )(hb, w)


def _matmul_nt_kernel(w_ref, x_ref, o_ref):
    o_ref[...] = lax.dot_general(w_ref[...], x_ref[...], (((1,), (1,)), ((), ())),
                                 preferred_element_type=F32).astype(o_ref.dtype)


def _v_projection_t(hb, wvt):
    n, d = hb.shape
    c = wvt.shape[0]
    t = ATTN_T
    return pl.pallas_call(
        _matmul_nt_kernel,
        grid=(n // t,),
        in_specs=[pl.BlockSpec((c, d), lambda i: (0, 0)),
                  pl.BlockSpec((t, d), lambda i: (i, 0))],
        out_specs=pl.BlockSpec((None, c, t), lambda i: (i, 0, 0)),
        out_shape=jax.ShapeDtypeStruct((n // t, c, t), BF16),
        compiler_params=_cparams(1),
        name="v_projection_t",
    )(wvt, hb)


def _rel_bucket(rel):
    n = jnp.maximum(rel, 0)
    max_exact = N_BUCKETS // 2
    nf = jnp.maximum(n, 1).astype(F32)
    large = max_exact + (jnp.log(nf / max_exact) / math.log(MAX_DISTANCE / max_exact)
                         * (N_BUCKETS - max_exact)).astype(jnp.int32)
    large = jnp.minimum(large, N_BUCKETS - 1)
    return jnp.where(n < max_exact, n, large)


def _bias_tiles(rel_table, t):
    assert t >= MAX_DISTANCE
    rel = jnp.arange(t, dtype=jnp.int32)[None, :] - jnp.arange(t, dtype=jnp.int32)[:, None]
    table = rel_table.astype(F32) * LOG2_E

    def lookup(r):
        onehot = (_rel_bucket(r)[:, :, None] == jnp.arange(N_BUCKETS, dtype=jnp.int32)).astype(F32)
        return jnp.einsum('kqb,bh->hkq', onehot, table, precision=lax.Precision.HIGHEST)

    diag = jnp.where((rel >= 0)[None], lookup(rel), MASK_VALUE)
    prev = lookup(rel + t)
    return jnp.stack([diag, prev], axis=1), table[N_BUCKETS - 1]


def _attn_kernel(lam_ref, far_ref, q_ref, k_ref, vt_ref, bias_ref, g_ref, o_ref,
                 m_ref, acc_ref, sa_ref, sb_ref, ma_ref, mb_ref, *, t, post_scale):
    head = pl.program_id(0)
    qi = pl.program_id(2)
    q = (q_ref[...].astype(F32) * (HEAD_DIM ** -0.5 * LOG2_E)).astype(BF16)
    lane = lax.broadcasted_iota(jnp.int32, q.shape, 1)
    zero = jnp.zeros_like(q)
    q_halves = (jnp.where(lane < HEAD_DIM, q, zero), jnp.where(lane >= HEAD_DIM, q, zero))
    ones = jnp.ones((BF16_SUBLANES, t), BF16)

    m_ref[...] = jnp.full(m_ref.shape, MASK_VALUE, F32)
    acc_ref[...] = jnp.zeros(acc_ref.shape, F32)

    buf_a = (sa_ref, ma_ref)
    buf_b = (sb_ref, mb_ref)

    def scores(j, buf):
        s_ref, smax_ref = buf
        k = k_ref[pl.ds(pl.multiple_of(j * t, t), t), :]
        for br in range(2):
            s = lax.dot_general(k, q_halves[br], (((1,), (1,)), ((), ())),
                                preferred_element_type=F32)
            s_ref[br] = s
            smax_ref[br] = jnp.max(s, axis=0, keepdims=True)

    def softmax_pv(j, buf, bias, shift):
        s_ref, smax_ref = buf
        v_aug = jnp.concatenate([vt_ref[j], ones], axis=0)
        for br in range(2):
            s = s_ref[br]
            m_prev = m_ref[br]
            if bias is None:
                m_tile = smax_ref[br] + shift
                m_new = jnp.maximum(m_prev, m_tile)
                p = jnp.exp2(s - (m_new - shift)).astype(BF16)
            else:
                s = s + bias
                m_new = jnp.maximum(m_prev, jnp.max(s, axis=0, keepdims=True))
                p = jnp.exp2(s - m_new).astype(BF16)
            alpha = jnp.exp2(m_prev - m_new)
            acc_ref[br] = alpha * acc_ref[br] + jnp.dot(v_aug, p, preferred_element_type=F32)
            m_ref[br] = m_new

    far_bias = far_ref[head]

    @pl.when(qi >= 1)
    def _():
        n_far = qi - 1
        odd = n_far % 2

        @pl.when(odd == 1)
        def _():
            scores(0, buf_b)
            scores(1, buf_a)
            softmax_pv(0, buf_b, None, far_bias)

        @pl.when(odd == 0)
        def _():
            scores(0, buf_a)

        def pair(i, carry):
            j = odd + 2 * i
            scores(j + 1, buf_b)
            softmax_pv(j, buf_a, None, far_bias)
            scores(j + 2, buf_a)
            softmax_pv(j + 1, buf_b, None, far_bias)
            return carry

        lax.fori_loop(0, n_far // 2, pair, 0)
        scores(qi, buf_b)
        softmax_pv(qi - 1, buf_a, bias_ref[1], None)

    @pl.when(qi == 0)
    def _():
        scores(0, buf_b)

    softmax_pv(qi, buf_b, bias_ref[0], None)

    a1 = acc_ref[0]
    a2 = acc_ref[1]
    o = a1[:V_DIM] / a1[V_DIM:V_DIM + 1] - lam_ref[0] * (a2[:V_DIM] / a2[V_DIM:V_DIM + 1])
    ms = jnp.mean(o * o, axis=0, keepdims=True)
    on = o * lax.rsqrt(ms + LN_EPS) * (g_ref[...] * post_scale)
    o_ref[...] = on.T.astype(o_ref.dtype)


def _diff_attention(z, vt, lam, bias, far_bias, subln_g, batch, post_scale):
    n = z.shape[0]
    seq = n // batch
    t = ATTN_T
    nq = seq // t
    d_attn = N_HEADS * V_DIM
    smem = pl.BlockSpec(memory_space=pltpu.SMEM)
    once = pl.Buffered(1)
    return pl.pallas_call(
        functools.partial(_attn_kernel, t=t, post_scale=post_scale),
        grid=(N_HEADS, batch, nq),
        in_specs=[
            smem, smem,
            pl.BlockSpec((t, V_DIM), lambda h, b, i: (b * nq + i, COL_Q * N_HEADS + h)),
            pl.BlockSpec((seq, V_DIM), lambda h, b, i: (b, COL_K * N_HEADS + h), pipeline_mode=once),
            pl.BlockSpec((nq, V_DIM, t), lambda h, b, i: (b, h, 0), pipeline_mode=once),
            pl.BlockSpec((None, 2, t, t), lambda h, b, i: (h, 0, 0, 0), pipeline_mode=once),
            pl.BlockSpec((V_DIM, 1), lambda h, b, i: (0, 0)),
        ],
        out_specs=pl.BlockSpec((t, V_DIM), lambda h, b, i: (b * nq + i, h)),
        out_shape=jax.ShapeDtypeStruct((n, d_attn), BF16),
        scratch_shapes=[pltpu.VMEM((2, 1, t), F32),
                        pltpu.VMEM((2, V_DIM + BF16_SUBLANES, t), F32),
                        pltpu.VMEM((2, t, t), F32), pltpu.VMEM((2, t, t), F32),
                        pltpu.VMEM((2, 1, t), F32), pltpu.VMEM((2, 1, t), F32)],
        compiler_params=_cparams(3),
        name="diff_attention",
    )(lam.reshape(1), far_bias, z, z, vt, bias, subln_g.reshape(V_DIM, 1))


def _mixer_tail_kernel(on_ref, cb_ref, cc_ref, cx_ref, ga_ref, gc_ref, cch_ref, cxh_ref, h_ref,
                       wa_ref, wc_ref, wo_ref, convw_ref, gb_ref, lg_ref, lb_ref, wr_ref, br_ref,
                       h1_ref, h1b_ref, logit_ref, *, seq, alpha):
    i = pl.program_id(0)
    tm = on_ref.shape[0]
    attn = jnp.dot(on_ref[...], wa_ref[...], preferred_element_type=F32)

    u = cc_ref[...].astype(F32) * cx_ref[...].astype(F32)
    halo = cch_ref[...].astype(F32) * cxh_ref[...].astype(F32)
    halo = jnp.where((i * tm) % seq == 0, 0.0, halo)
    hm1 = halo[BF16_SUBLANES - 1:BF16_SUBLANES]
    hm2 = halo[BF16_SUBLANES - 2:BF16_SUBLANES - 1]
    row = lax.broadcasted_iota(jnp.int32, u.shape, 0)
    u1 = jnp.where(row == 0, hm1, pltpu.roll(u, 1, axis=0))
    u2 = jnp.where(row == 0, hm2, jnp.where(row == 1, hm1, pltpu.roll(u, 2, axis=0)))
    w = convw_ref[...]
    y = w[2:3] * u + w[1:2] * u1 + w[0:1] * u2
    conv = jnp.dot((cb_ref[...].astype(F32) * y).astype(BF16), wc_ref[...],
                   preferred_element_type=F32)

    gb = gb_ref[...]
    merged = (jax.nn.sigmoid(ga_ref[...].astype(F32) + gb[0:1]) * attn
              + jax.nn.sigmoid(gc_ref[...].astype(F32) + gb[1:2]) * conv)
    mix = jnp.dot(merged.astype(BF16), wo_ref[...], preferred_element_type=F32)
    h1 = _layer_norm(alpha * h_ref[...] + mix, lg_ref[...], lb_ref[...])
    h1b = h1.astype(BF16)
    h1_ref[...] = h1
    h1b_ref[...] = h1b
    logit_ref[...] = jnp.dot(h1b, wr_ref[...], preferred_element_type=F32) + br_ref[...]


def _mixer_tail(on, z, h, wa, wc, wo, conv_w, gate_b, ln_g, ln_b, wr, br, seq, alpha):
    n, d = h.shape
    tm = LN_ROWS
    halo_blocks = tm // BF16_SUBLANES

    def col(c):
        return pl.BlockSpec((tm, d), lambda i: (i, c))

    def halo(c):
        return pl.BlockSpec((BF16_SUBLANES, d), lambda i: (jnp.maximum(i * halo_blocks - 1, 0), c))

    def full(shape):
        return pl.BlockSpec(shape, lambda i: (0,) * len(shape))

    row = pl.BlockSpec((tm, d), lambda i: (i, 0))
    return pl.pallas_call(
        functools.partial(_mixer_tail_kernel, seq=seq, alpha=alpha),
        grid=(n // tm,),
        in_specs=[row, col(COL_CB), col(COL_CC), col(COL_CX), col(COL_GA), col(COL_GC),
                  halo(COL_CC), halo(COL_CX), row,
                  full((d, d)), full((d, d)), full((d, d)), full((CONV_K, d)), full((2, d)),
                  full((1, d)), full((1, d)), full((d, ROUTER_LANES)), full((1, ROUTER_LANES))],
        out_specs=[row, row, pl.BlockSpec((tm, ROUTER_LANES), lambda i: (i, 0))],
        out_shape=[jax.ShapeDtypeStruct((n, d), F32), jax.ShapeDtypeStruct((n, d), BF16),
                   jax.ShapeDtypeStruct((n, ROUTER_LANES), F32)],
        compiler_params=_cparams(1),
        name="mixer_tail",
    )(on, z, z, z, z, z, z, z, h, wa, wc, wo, conv_w, gate_b,
      ln_g.reshape(1, d), ln_b.reshape(1, d), wr, br)


def _route(logits, n_blk):
    n_tok = logits.shape[0]
    n_asg = n_tok * TOP_K
    gl = logits[:, :N_GROUPS]
    g_max = jnp.max(gl, axis=-1, keepdims=True)
    g_idx = jnp.argmax(gl, axis=-1)[:, None].astype(jnp.int32)
    g_p = 1.0 / jnp.sum(jnp.exp(gl - g_max), axis=-1, keepdims=True)
    e_all = logits[:, N_GROUPS:N_GROUPS + N_EXPERTS].reshape(n_tok, N_GROUPS, EXPERTS_PER_GROUP)
    in_group = g_idx[:, :, None] == jnp.arange(N_GROUPS, dtype=jnp.int32)[None, :, None]
    e_logits = jnp.sum(jnp.where(in_group, e_all, 0.0), axis=1)
    loc = jnp.arange(EXPERTS_PER_GROUP, dtype=jnp.int32)[None, :]
    i0 = jnp.argmax(e_logits, axis=-1)[:, None].astype(jnp.int32)
    e0 = jnp.max(e_logits, axis=-1, keepdims=True)
    rest = jnp.where(loc == i0, -jnp.inf, e_logits)
    i1 = jnp.argmax(rest, axis=-1)[:, None].astype(jnp.int32)
    e1 = jnp.max(rest, axis=-1, keepdims=True)
    r = jnp.exp(e1 - e0)
    gate = jnp.concatenate([g_p / (1.0 + r), g_p * r / (1.0 + r)], axis=-1)
    e_loc = jnp.concatenate([i0, i1], axis=-1)
    expert = (g_idx * EXPERTS_PER_GROUP + e_loc).reshape(-1).astype(jnp.int32)

    onehot = (expert[:, None] == jnp.arange(N_EXPERTS, dtype=jnp.int32)[None, :]).astype(jnp.int32)
    chunk = EXPERT_ROWS
    tri = jnp.tril(jnp.ones((chunk, chunk), BF16))
    local = jnp.einsum('ij,cjk->cik', tri, onehot.astype(BF16).reshape(n_asg // chunk, chunk, N_EXPERTS),
                       preferred_element_type=F32).astype(jnp.int32)
    totals = local[:, -1, :]
    csum = (local + (jnp.cumsum(totals, axis=0) - totals)[:, None, :]).reshape(n_asg, N_EXPERTS)
    sizes = csum[-1]
    padded = (sizes + EXPERT_ROWS - 1) // EXPERT_ROWS * EXPERT_ROWS
    pad_end = jnp.cumsum(padded)
    pad_start = pad_end - padded
    dest = jnp.sum(onehot * (csum - 1 + pad_start[None, :]), axis=1)
    src_tok = (jnp.arange(n_blk * EXPERT_ROWS, dtype=jnp.int32) % n_tok).at[dest].set(
        jnp.arange(n_asg, dtype=jnp.int32) // TOP_K, unique_indices=True, mode="promise_in_bounds")
    blk_row = jnp.arange(n_blk, dtype=jnp.int32) * EXPERT_ROWS
    blk_e = jnp.minimum(jnp.sum((pad_end[None, :] <= blk_row[:, None]).astype(jnp.int32), axis=1),
                        N_EXPERTS - 1)
    n_used = (pad_end[-1] // EXPERT_ROWS).astype(jnp.int32).reshape(1)
    return gate, dest.reshape(n_tok, TOP_K), src_tok, blk_e, n_used


def _expert_kernel(blk_e_ref, n_used_ref, x_ref, w1_ref, w3_ref, w2_ref, y_ref,
                   w1b_ref, w3b_ref, w2b_ref):
    i = pl.program_id(0)
    new_expert = jnp.logical_or(i == 0, blk_e_ref[i] != blk_e_ref[jnp.maximum(i - 1, 0)])

    @pl.when(jnp.logical_and(i < n_used_ref[0], new_expert))
    def _():
        w1b_ref[...] = w1_ref[...].astype(BF16)
        w3b_ref[...] = w3_ref[...].astype(BF16)
        w2b_ref[...] = w2_ref[...].astype(BF16)

    @pl.when(i < n_used_ref[0])
    def _():
        x = x_ref[...]
        a = jnp.dot(x, w1b_ref[...], preferred_element_type=F32)
        b = jnp.dot(x, w3b_ref[...], preferred_element_type=F32)
        mid = (a * jax.nn.sigmoid(a) * b).astype(BF16)
        y_ref[...] = jnp.dot(mid, w2b_ref[...], preferred_element_type=F32).astype(y_ref.dtype)

    @pl.when(i >= n_used_ref[0])
    def _():
        y_ref[...] = jnp.zeros(y_ref.shape, y_ref.dtype)


def _expert_mlp(xbuf, blk_e, n_used, w1, w3, w2, layer):
    n_rows, d = xbuf.shape
    de = w1.shape[-1]
    n_blk = n_rows // EXPERT_ROWS
    grid_spec = pltpu.PrefetchScalarGridSpec(
        num_scalar_prefetch=2,
        grid=(n_blk,),
        in_specs=[
            pl.BlockSpec((EXPERT_ROWS, d), lambda i, be, nu: (jnp.minimum(i, nu[0] - 1), 0)),
            pl.BlockSpec((None, None, d, de), lambda i, be, nu: (layer, be[i], 0, 0)),
            pl.BlockSpec((None, None, d, de), lambda i, be, nu: (layer, be[i], 0, 0)),
            pl.BlockSpec((None, None, de, d), lambda i, be, nu: (layer, be[i], 0, 0)),
        ],
        out_specs=pl.BlockSpec((EXPERT_ROWS, d), lambda i, be, nu: (i, 0)),
        scratch_shapes=[pltpu.VMEM((d, de), BF16), pltpu.VMEM((d, de), BF16),
                        pltpu.VMEM((de, d), BF16)],
    )
    return pl.pallas_call(
        _expert_kernel,
        grid_spec=grid_spec,
        out_shape=jax.ShapeDtypeStruct((n_rows, d), BF16),
        compiler_params=_cparams(1),
        name="expert_mlp",
    )(blk_e, n_used, xbuf, w1, w3, w2)


def _combine_kernel(h_ref, y0_ref, y1_ref, gate_ref, g_ref, b_ref, o_ref, ob_ref, *, alpha):
    gate = gate_ref[...]
    ffn = gate[:, 0:1] * y0_ref[...].astype(F32) + gate[:, 1:2] * y1_ref[...].astype(F32)
    h2 = _layer_norm(alpha * h_ref[...] + ffn, g_ref[...], b_ref[...])
    o_ref[...] = h2
    ob_ref[...] = h2.astype(BF16)


def _combine(h, y_both, gate, ln_g, ln_b, alpha):
    n, d = h.shape
    n_tiles = n // LN_ROWS
    row = pl.BlockSpec((LN_ROWS, d), lambda i: (i, 0))
    vec = pl.BlockSpec((1, d), lambda i: (0, 0))
    return pl.pallas_call(
        functools.partial(_combine_kernel, alpha=alpha),
        grid=(n // LN_ROWS,),
        in_specs=[row, row, pl.BlockSpec((LN_ROWS, d), lambda i: (i + n_tiles, 0)),
                  pl.BlockSpec((LN_ROWS, TOP_K), lambda i: (i, 0)), vec, vec],
        out_specs=[row, row],
        out_shape=[jax.ShapeDtypeStruct((n, d), F32), jax.ShapeDtypeStruct((n, d), BF16)],
        compiler_params=_cparams(1),
        name="moe_combine",
    )(h, y_both, y_both, gate, ln_g.reshape(1, d), ln_b.reshape(1, d))


def kernel(x, ln0_g, ln0_b, rel_table, w_in, gate_b, lam_vecs, subln_g, w_attn_proj, conv_w,
           w_conv_proj, w_out, ln1_g, ln1_b, w_rg, b_rg, w_re, b_re, w1, w3, w2, ln2_g, ln2_b):
    batch, seq, d = x.shape
    depth = w_in.shape[0]
    n_tok = batch * seq
    alpha = (2 * depth) ** 0.25
    assert seq % ATTN_T == 0 and n_tok % PROJ_TM == 0 and seq % LN_ROWS == 0
    n_asg = n_tok * TOP_K
    n_blk = -(-(n_asg + N_EXPERTS * (EXPERT_ROWS - 1)) // EXPERT_ROWS)

    bias, far_bias = _bias_tiles(rel_table, ATTN_T)
    h, hb = _input_layer_norm(x.reshape(n_tok, d), ln0_g, ln0_b)
    for l in range(depth):
        lam_init = 0.8 - 0.6 * math.exp(-0.3 * l)
        lv = lam_vecs[l].astype(F32)
        lam = jnp.exp(jnp.sum(lv[0] * lv[1])) - jnp.exp(jnp.sum(lv[2] * lv[3])) + lam_init

        w_l = w_in[l]
        w_tok = jnp.concatenate([w_l[:, :IN_V * d], w_l[:, IN_CB * d:]], axis=1).astype(BF16)
        w_vt = w_l[:, IN_V * d:IN_CB * d].T.astype(BF16)
        z = _in_projection(hb, w_tok)
        vt = _v_projection_t(hb, w_vt)
        on = _diff_attention(z, vt, lam, bias, far_bias, subln_g[l], batch, 1.0 - lam_init)
        w_router = jnp.zeros((d, ROUTER_LANES), F32)
        w_router = w_router.at[:, :N_GROUPS].set(w_rg[l]).at[:, N_GROUPS:N_GROUPS + N_EXPERTS].set(w_re[l])
        b_router = jnp.zeros((1, ROUTER_LANES), F32)
        b_router = b_router.at[0, :N_GROUPS].set(b_rg[l]).at[0, N_GROUPS:N_GROUPS + N_EXPERTS].set(b_re[l])
        h1, h1b, logits = _mixer_tail(
            on, z, h, w_attn_proj[l].astype(BF16), w_conv_proj[l].astype(BF16), w_out[l].astype(BF16),
            conv_w[l], gate_b[l], ln1_g[l], ln1_b[l], w_router.astype(BF16), b_router, seq, alpha)

        gate, dest, src_tok, blk_e, n_used = _route(logits, n_blk)
        xbuf = jnp.take(h1b, src_tok, axis=0, mode="clip")
        ybuf = _expert_mlp(xbuf, blk_e, n_used, w1, w3, w2, l)
        y_both = jnp.take(ybuf, jnp.concatenate([dest[:, 0], dest[:, 1]]), axis=0,
                          mode="clip")
        h, hb = _combine(h1, y_both, gate, ln2_g[l], ln2_b[l], alpha)
    return h.reshape(batch, seq, d)
```

```python
import functools
import math

import jax
import jax.numpy as jnp
from jax import lax
from jax.experimental import pallas as pl
from jax.experimental.pallas import tpu as pltpu

F32 = jnp.float32
BF16 = jnp.bfloat16

N_HEADS = 8
HEAD_DIM = 64
V_DIM = 2 * HEAD_DIM
N_BUCKETS = 32
MAX_DISTANCE = 128
CONV_K = 3
N_GROUPS = 4
EXPERTS_PER_GROUP = 8
N_EXPERTS = N_GROUPS * EXPERTS_PER_GROUP
TOP_K = 2
LN_EPS = 1e-5
MASK_VALUE = -1e30
LOG2_E = math.log2(math.e)

IN_Q, IN_K, IN_V, IN_CB, IN_CC, IN_CX, IN_GA, IN_GC = range(8)
COL_Q, COL_K, COL_CB, COL_CC, COL_CX, COL_GA, COL_GC = range(7)

LANES = 128
BF16_SUBLANES = 16
ROUTER_LANES = LANES

LN_ROWS = 512
PROJ_TM = 1024
PROJ_TN = 1024
ATTN_T = 1024
EXPERT_ROWS = 512
VMEM_LIMIT = 56 * 1024 * 1024


def _cparams(n_axes, flags=None):
    return pltpu.CompilerParams(dimension_semantics=("arbitrary",) * n_axes,
                                vmem_limit_bytes=VMEM_LIMIT, flags=flags)


def _layer_norm(x, g, b):
    mu = jnp.mean(x, axis=-1, keepdims=True)
    xc = x - mu
    var = jnp.mean(xc * xc, axis=-1, keepdims=True)
    return xc * lax.rsqrt(var + LN_EPS) * g + b


def _ln_kernel(x_ref, g_ref, b_ref, h_ref, hb_ref):
    h = _layer_norm(x_ref[...], g_ref[...], b_ref[...])
    h_ref[...] = h
    hb_ref[...] = h.astype(BF16)


def _input_layer_norm(x, g, b):
    n, d = x.shape
    row = pl.BlockSpec((LN_ROWS, d), lambda i: (i, 0))
    vec = pl.BlockSpec((1, d), lambda i: (0, 0))
    return pl.pallas_call(
        _ln_kernel,
        grid=(n // LN_ROWS,),
        in_specs=[row, vec, vec],
        out_specs=[row, row],
        out_shape=[jax.ShapeDtypeStruct((n, d), F32), jax.ShapeDtypeStruct((n, d), BF16)],
        compiler_params=_cparams(1),
        name="input_layer_norm",
    )(x, g.reshape(1, d), b.reshape(1, d))


def _matmul_kernel(x_ref, w_ref, o_ref):
    o_ref[...] = jnp.dot(x_ref[...], w_ref[...], preferred_element_type=F32).astype(o_ref.dtype)


def _in_projection(hb, w):
    n, d = hb.shape
    c = w.shape[1]
    return pl.pallas_call(
        _matmul_kernel,
        grid=(n // PROJ_TM, c // PROJ_TN),
        in_specs=[pl.BlockSpec((PROJ_TM, d), lambda i, j: (i, 0)),
                  pl.BlockSpec((d, PROJ_TN), lambda i, j: (0, j))],
        out_specs=pl.BlockSpec((PROJ_TM, PROJ_TN), lambda i, j: (i, j)),
        out_shape=jax.ShapeDtypeStruct((n, c), BF16),
        compiler_params=_cparams(2),
        name="in_projection",
    )(hb, w)


def _matmul_nt_kernel(w_ref, x_ref, o_ref):
    o_ref[...] = lax.dot_general(w_ref[...], x_ref[...], (((1,), (1,)), ((), ())),
                                 preferred_element_type=F32).astype(o_ref.dtype)


def _v_projection_t(hb, wvt):
    n, d = hb.shape
    c = wvt.shape[0]
    t = ATTN_T
    return pl.pallas_call(
        _matmul_nt_kernel,
        grid=(n // t,),
        in_specs=[pl.BlockSpec((c, d), lambda i: (0, 0)),
                  pl.BlockSpec((t, d), lambda i: (i, 0))],
        out_specs=pl.BlockSpec((None, c, t), lambda i: (i, 0, 0)),
        out_shape=jax.ShapeDtypeStruct((n // t, c, t), BF16),
        compiler_params=_cparams(1),
        name="v_projection_t",
    )(wvt, hb)


def _rel_bucket(rel):
    n = jnp.maximum(rel, 0)
    max_exact = N_BUCKETS // 2
    nf = jnp.maximum(n, 1).astype(F32)
    large = max_exact + (jnp.log(nf / max_exact) / math.log(MAX_DISTANCE / max_exact)
                         * (N_BUCKETS - max_exact)).astype(jnp.int32)
    large = jnp.minimum(large, N_BUCKETS - 1)
    return jnp.where(n < max_exact, n, large)


def _bias_tiles(rel_table, t):
    assert t >= MAX_DISTANCE
    rel = jnp.arange(t, dtype=jnp.int32)[None, :] - jnp.arange(t, dtype=jnp.int32)[:, None]
    table = rel_table.astype(F32) * LOG2_E

    def lookup(r):
        onehot = (_rel_bucket(r)[:, :, None] == jnp.arange(N_BUCKETS, dtype=jnp.int32)).astype(F32)
        return jnp.einsum('kqb,bh->hkq', onehot, table, precision=lax.Precision.HIGHEST)

    diag = jnp.where((rel >= 0)[None], lookup(rel), MASK_VALUE)
    prev = lookup(rel + t)
    return jnp.stack([diag, prev], axis=1), table[N_BUCKETS - 1]


def _attn_kernel(lam_ref, far_ref, q_ref, k_ref, vt_ref, bias_ref, g_ref, o_ref,
                 m_ref, acc_ref, sa_ref, sb_ref, ma_ref, mb_ref, *, t, post_scale):
    head = pl.program_id(0)
    qi = pl.program_id(2)
    q = (q_ref[...].astype(F32) * (HEAD_DIM ** -0.5 * LOG2_E)).astype(BF16)
    lane = lax.broadcasted_iota(jnp.int32, q.shape, 1)
    zero = jnp.zeros_like(q)
    q_halves = (jnp.where(lane < HEAD_DIM, q, zero), jnp.where(lane >= HEAD_DIM, q, zero))
    ones = jnp.ones((BF16_SUBLANES, t), BF16)

    m_ref[...] = jnp.full(m_ref.shape, MASK_VALUE, F32)
    acc_ref[...] = jnp.zeros(acc_ref.shape, F32)

    buf_a = (sa_ref, ma_ref)
    buf_b = (sb_ref, mb_ref)

    def scores(j, buf):
        s_ref, smax_ref = buf
        k = k_ref[pl.ds(pl.multiple_of(j * t, t), t), :]
        for br in range(2):
            s = lax.dot_general(k, q_halves[br], (((1,), (1,)), ((), ())),
                                preferred_element_type=F32)
            s_ref[br] = s
            smax_ref[br] = jnp.max(s, axis=0, keepdims=True)

    def probs(buf, bias, shift):
        s_ref, smax_ref = buf
        out = []
        for br in range(2):
            s = s_ref[br]
            m_prev = m_ref[br]
            if bias is None:
                m_tile = smax_ref[br] + shift
                m_new = jnp.maximum(m_prev, m_tile)
                p = jnp.exp2(s - (m_new - shift)).astype(BF16)
            else:
                s = s + bias
                m_new = jnp.maximum(m_prev, jnp.max(s, axis=0, keepdims=True))
                p = jnp.exp2(s - m_new).astype(BF16)
            out.append((p, jnp.exp2(m_prev - m_new)))
            m_ref[br] = m_new
        return out

    def accumulate(j, tile_probs):
        v_aug = jnp.concatenate([vt_ref[j], ones], axis=0)
        for br, (p, alpha) in enumerate(tile_probs):
            acc_ref[br] = alpha * acc_ref[br] + jnp.dot(v_aug, p, preferred_element_type=F32)

    far_bias = far_ref[head]

    @pl.when(qi >= 1)
    def _():
        n_far = qi - 1
        odd = n_far % 2

        @pl.when(odd == 1)
        def _():
            scores(0, buf_b)
            tile_probs = probs(buf_b, None, far_bias)
            scores(1, buf_a)
            accumulate(0, tile_probs)

        @pl.when(odd == 0)
        def _():
            scores(0, buf_a)

        def pair(i, carry):
            j = odd + 2 * i
            tile_probs = probs(buf_a, None, far_bias)
            scores(j + 1, buf_b)
            accumulate(j, tile_probs)
            tile_probs = probs(buf_b, None, far_bias)
            scores(j + 2, buf_a)
            accumulate(j + 1, tile_probs)
            return carry

        lax.fori_loop(0, n_far // 2, pair, 0)
        tile_probs = probs(buf_a, bias_ref[1], None)
        scores(qi, buf_b)
        accumulate(qi - 1, tile_probs)

    @pl.when(qi == 0)
    def _():
        scores(0, buf_b)

    accumulate(qi, probs(buf_b, bias_ref[0], None))

    a1 = acc_ref[0]
    a2 = acc_ref[1]
    o = a1[:V_DIM] / a1[V_DIM:V_DIM + 1] - lam_ref[0] * (a2[:V_DIM] / a2[V_DIM:V_DIM + 1])
    ms = jnp.mean(o * o, axis=0, keepdims=True)
    on = o * lax.rsqrt(ms + LN_EPS) * (g_ref[...] * post_scale)
    o_ref[...] = on.T.astype(o_ref.dtype)


def _diff_attention(z, vt, lam, bias, far_bias, subln_g, batch, post_scale):
    n = z.shape[0]
    seq = n // batch
    t = ATTN_T
    nq = seq // t
    d_attn = N_HEADS * V_DIM
    smem = pl.BlockSpec(memory_space=pltpu.SMEM)
    once = pl.Buffered(1)
    return pl.pallas_call(
        functools.partial(_attn_kernel, t=t, post_scale=post_scale),
        grid=(N_HEADS, batch, nq),
        in_specs=[
            smem, smem,
            pl.BlockSpec((t, V_DIM), lambda h, b, i: (b * nq + i, COL_Q * N_HEADS + h)),
            pl.BlockSpec((seq, V_DIM), lambda h, b, i: (b, COL_K * N_HEADS + h), pipeline_mode=once),
            pl.BlockSpec((nq, V_DIM, t), lambda h, b, i: (b, h, 0), pipeline_mode=once),
            pl.BlockSpec((None, 2, t, t), lambda h, b, i: (h, 0, 0, 0), pipeline_mode=once),
            pl.BlockSpec((V_DIM, 1), lambda h, b, i: (0, 0)),
        ],
        out_specs=pl.BlockSpec((t, V_DIM), lambda h, b, i: (b * nq + i, h)),
        out_shape=jax.ShapeDtypeStruct((n, d_attn), BF16),
        scratch_shapes=[pltpu.VMEM((2, 1, t), F32),
                        pltpu.VMEM((2, V_DIM + BF16_SUBLANES, t), F32),
                        pltpu.VMEM((2, t, t), F32), pltpu.VMEM((2, t, t), F32),
                        pltpu.VMEM((2, 1, t), F32), pltpu.VMEM((2, 1, t), F32)],
        compiler_params=_cparams(3),
        name="diff_attention",
    )(lam.reshape(1), far_bias, z, z, vt, bias, subln_g.reshape(V_DIM, 1))


def _mixer_tail_kernel(on_ref, cb_ref, cc_ref, cx_ref, ga_ref, gc_ref, cch_ref, cxh_ref, h_ref,
                       wa_ref, wc_ref, wo_ref, convw_ref, gb_ref, lg_ref, lb_ref, wr_ref, br_ref,
                       h1_ref, h1b_ref, logit_ref, *, seq, alpha):
    i = pl.program_id(0)
    tm = on_ref.shape[0]
    attn = jnp.dot(on_ref[...], wa_ref[...], preferred_element_type=F32)

    u = cc_ref[...].astype(F32) * cx_ref[...].astype(F32)
    halo = cch_ref[...].astype(F32) * cxh_ref[...].astype(F32)
    halo = jnp.where((i * tm) % seq == 0, 0.0, halo)
    hm1 = halo[BF16_SUBLANES - 1:BF16_SUBLANES]
    hm2 = halo[BF16_SUBLANES - 2:BF16_SUBLANES - 1]
    row = lax.broadcasted_iota(jnp.int32, u.shape, 0)
    u1 = jnp.where(row == 0, hm1, pltpu.roll(u, 1, axis=0))
    u2 = jnp.where(row == 0, hm2, jnp.where(row == 1, hm1, pltpu.roll(u, 2, axis=0)))
    w = convw_ref[...]
    y = w[2:3] * u + w[1:2] * u1 + w[0:1] * u2
    conv = jnp.dot((cb_ref[...].astype(F32) * y).astype(BF16), wc_ref[...],
                   preferred_element_type=F32)

    gb = gb_ref[...]
    merged = (jax.nn.sigmoid(ga_ref[...].astype(F32) + gb[0:1]) * attn
              + jax.nn.sigmoid(gc_ref[...].astype(F32) + gb[1:2]) * conv)
    mix = jnp.dot(merged.astype(BF16), wo_ref[...], preferred_element_type=F32)
    h1 = _layer_norm(alpha * h_ref[...] + mix, lg_ref[...], lb_ref[...])
    h1b = h1.astype(BF16)
    h1_ref[...] = h1
    h1b_ref[...] = h1b
    logits = jnp.dot(h1b, wr_ref[...], preferred_element_type=F32) + br_ref[...]
    logit_ref[...] = logits.T


def _mixer_tail(on, z, h, wa, wc, wo, conv_w, gate_b, ln_g, ln_b, wr, br, seq, alpha):
    n, d = h.shape
    tm = LN_ROWS
    halo_blocks = tm // BF16_SUBLANES

    def col(c):
        return pl.BlockSpec((tm, d), lambda i: (i, c))

    def halo(c):
        return pl.BlockSpec((BF16_SUBLANES, d), lambda i: (jnp.maximum(i * halo_blocks - 1, 0), c))

    def full(shape):
        return pl.BlockSpec(shape, lambda i: (0,) * len(shape))

    row = pl.BlockSpec((tm, d), lambda i: (i, 0))
    return pl.pallas_call(
        functools.partial(_mixer_tail_kernel, seq=seq, alpha=alpha),
        grid=(n // tm,),
        in_specs=[row, col(COL_CB), col(COL_CC), col(COL_CX), col(COL_GA), col(COL_GC),
                  halo(COL_CC), halo(COL_CX), row,
                  full((d, d)), full((d, d)), full((d, d)), full((CONV_K, d)), full((2, d)),
                  full((1, d)), full((1, d)), full((d, ROUTER_LANES)), full((1, ROUTER_LANES))],
        out_specs=[row, row, pl.BlockSpec((ROUTER_LANES, tm), lambda i: (0, i))],
        out_shape=[jax.ShapeDtypeStruct((n, d), F32), jax.ShapeDtypeStruct((n, d), BF16),
                   jax.ShapeDtypeStruct((ROUTER_LANES, n), F32)],
        compiler_params=_cparams(1),
        name="mixer_tail",
    )(on, z, z, z, z, z, z, z, h, wa, wc, wo, conv_w, gate_b,
      ln_g.reshape(1, d), ln_b.reshape(1, d), wr, br)


def _route(logits_t, n_blk):
    n_tok = logits_t.shape[1]
    n_asg = n_tok * TOP_K
    gl = logits_t[:N_GROUPS]
    g_max = jnp.max(gl, axis=0, keepdims=True)
    g_idx = jnp.argmax(gl, axis=0).astype(jnp.int32)
    g_p = 1.0 / jnp.sum(jnp.exp(gl - g_max), axis=0)
    e_all = logits_t[N_GROUPS:N_GROUPS + N_EXPERTS].reshape(N_GROUPS, EXPERTS_PER_GROUP, n_tok)
    in_group = g_idx[None, None, :] == jnp.arange(N_GROUPS, dtype=jnp.int32)[:, None, None]
    e_logits = jnp.sum(jnp.where(in_group, e_all, 0.0), axis=0)
    loc = jnp.arange(EXPERTS_PER_GROUP, dtype=jnp.int32)[:, None]
    i0 = jnp.argmax(e_logits, axis=0).astype(jnp.int32)
    e0 = jnp.max(e_logits, axis=0)
    rest = jnp.where(loc == i0[None, :], -jnp.inf, e_logits)
    i1 = jnp.argmax(rest, axis=0).astype(jnp.int32)
    e1 = jnp.max(rest, axis=0)
    r = jnp.exp(e1 - e0)
    gate = jnp.stack([g_p / (1.0 + r), g_p * r / (1.0 + r)], axis=1)
    expert = jnp.concatenate([g_idx * EXPERTS_PER_GROUP + i0, g_idx * EXPERTS_PER_GROUP + i1])

    onehot = expert[None, :] == jnp.arange(N_EXPERTS, dtype=jnp.int32)[:, None]
    chunk = EXPERT_ROWS
    triu = jnp.triu(jnp.ones((chunk, chunk), BF16))
    local = jnp.einsum('eck,kj->ecj', onehot.astype(BF16).reshape(N_EXPERTS, n_asg // chunk, chunk),
                       triu, preferred_element_type=F32).astype(jnp.int32)
    totals = local[:, :, -1]
    csum = (local + (jnp.cumsum(totals, axis=1) - totals)[:, :, None]).reshape(N_EXPERTS, n_asg)
    sizes = csum[:, -1]
    padded = (sizes + EXPERT_ROWS - 1) // EXPERT_ROWS * EXPERT_ROWS
    pad_end = jnp.cumsum(padded)
    pad_start = pad_end - padded
    dest = jnp.sum(jnp.where(onehot, csum - 1 + pad_start[:, None], 0), axis=0)
    src_tok = (jnp.arange(n_blk * EXPERT_ROWS, dtype=jnp.int32) % n_tok).at[dest].set(
        jnp.arange(n_asg, dtype=jnp.int32) % n_tok, unique_indices=True, mode="promise_in_bounds")
    blk_row = jnp.arange(n_blk, dtype=jnp.int32) * EXPERT_ROWS
    blk_e = jnp.minimum(jnp.sum((pad_end[None, :] <= blk_row[:, None]).astype(jnp.int32), axis=1),
                        N_EXPERTS - 1)
    n_used = (pad_end[-1] // EXPERT_ROWS).astype(jnp.int32).reshape(1)
    return gate, dest, src_tok, blk_e, n_used


def _expert_kernel(blk_e_ref, n_used_ref, x_ref, w1_ref, w3_ref, w2_ref, y_ref,
                   w1b_ref, w3b_ref, w2b_ref):
    i = pl.program_id(0)
    new_expert = jnp.logical_or(i == 0, blk_e_ref[i] != blk_e_ref[jnp.maximum(i - 1, 0)])

    @pl.when(jnp.logical_and(i < n_used_ref[0], new_expert))
    def _():
        w1b_ref[...] = w1_ref[...].astype(BF16)
        w3b_ref[...] = w3_ref[...].astype(BF16)
        w2b_ref[...] = w2_ref[...].astype(BF16)

    @pl.when(i < n_used_ref[0])
    def _():
        x = x_ref[...]
        a = jnp.dot(x, w1b_ref[...], preferred_element_type=F32)
        b = jnp.dot(x, w3b_ref[...], preferred_element_type=F32)
        mid = (a * jax.nn.sigmoid(a) * b).astype(BF16)
        y_ref[...] = jnp.dot(mid, w2b_ref[...], preferred_element_type=F32).astype(y_ref.dtype)

    @pl.when(i >= n_used_ref[0])
    def _():
        y_ref[...] = jnp.zeros(y_ref.shape, y_ref.dtype)


def _expert_mlp(xbuf, blk_e, n_used, w1, w3, w2, layer):
    n_rows, d = xbuf.shape
    de = w1.shape[-1]
    n_blk = n_rows // EXPERT_ROWS
    grid_spec = pltpu.PrefetchScalarGridSpec(
        num_scalar_prefetch=2,
        grid=(n_blk,),
        in_specs=[
            pl.BlockSpec((EXPERT_ROWS, d), lambda i, be, nu: (jnp.minimum(i, nu[0] - 1), 0)),
            pl.BlockSpec((None, None, d, de), lambda i, be, nu: (layer, be[i], 0, 0)),
            pl.BlockSpec((None, None, d, de), lambda i, be, nu: (layer, be[i], 0, 0)),
            pl.BlockSpec((None, None, de, d), lambda i, be, nu: (layer, be[i], 0, 0)),
        ],
        out_specs=pl.BlockSpec((EXPERT_ROWS, d), lambda i, be, nu: (i, 0)),
        scratch_shapes=[pltpu.VMEM((d, de), BF16), pltpu.VMEM((d, de), BF16),
                        pltpu.VMEM((de, d), BF16)],
    )
    return pl.pallas_call(
        _expert_kernel,
        grid_spec=grid_spec,
        out_shape=jax.ShapeDtypeStruct((n_rows, d), BF16),
        compiler_params=_cparams(1),
        name="expert_mlp",
    )(blk_e, n_used, xbuf, w1, w3, w2)


def _combine_kernel(h_ref, y0_ref, y1_ref, gate_ref, g_ref, b_ref, o_ref, ob_ref, *, alpha):
    gate = gate_ref[...]
    ffn = gate[:, 0:1] * y0_ref[...].astype(F32) + gate[:, 1:2] * y1_ref[...].astype(F32)
    h2 = _layer_norm(alpha * h_ref[...] + ffn, g_ref[...], b_ref[...])
    o_ref[...] = h2
    ob_ref[...] = h2.astype(BF16)


def _combine(h, y_both, gate, ln_g, ln_b, alpha):
    n, d = h.shape
    n_tiles = n // LN_ROWS
    row = pl.BlockSpec((LN_ROWS, d), lambda i: (i, 0))
    vec = pl.BlockSpec((1, d), lambda i: (0, 0))
    return pl.pallas_call(
        functools.partial(_combine_kernel, alpha=alpha),
        grid=(n // LN_ROWS,),
        in_specs=[row, row, pl.BlockSpec((LN_ROWS, d), lambda i: (i + n_tiles, 0)),
                  pl.BlockSpec((LN_ROWS, TOP_K), lambda i: (i, 0)), vec, vec],
        out_specs=[row, row],
        out_shape=[jax.ShapeDtypeStruct((n, d), F32), jax.ShapeDtypeStruct((n, d), BF16)],
        compiler_params=_cparams(1),
        name="moe_combine",
    )(h, y_both, y_both, gate, ln_g.reshape(1, d), ln_b.reshape(1, d))


def kernel(x, ln0_g, ln0_b, rel_table, w_in, gate_b, lam_vecs, subln_g, w_attn_proj, conv_w,
           w_conv_proj, w_out, ln1_g, ln1_b, w_rg, b_rg, w_re, b_re, w1, w3, w2, ln2_g, ln2_b):
    batch, seq, d = x.shape
    depth = w_in.shape[0]
    n_tok = batch * seq
    alpha = (2 * depth) ** 0.25
    assert seq % ATTN_T == 0 and n_tok % PROJ_TM == 0 and seq % LN_ROWS == 0
    n_asg = n_tok * TOP_K
    n_blk = -(-(n_asg + N_EXPERTS * (EXPERT_ROWS - 1)) // EXPERT_ROWS)

    bias, far_bias = _bias_tiles(rel_table, ATTN_T)
    h, hb = _input_layer_norm(x.reshape(n_tok, d), ln0_g, ln0_b)
    for l in range(depth):
        lam_init = 0.8 - 0.6 * math.exp(-0.3 * l)
        lv = lam_vecs[l].astype(F32)
        lam = jnp.exp(jnp.sum(lv[0] * lv[1])) - jnp.exp(jnp.sum(lv[2] * lv[3])) + lam_init

        w_l = w_in[l]
        w_tok = jnp.concatenate([w_l[:, :IN_V * d], w_l[:, IN_CB * d:]], axis=1).astype(BF16)
        w_vt = w_l[:, IN_V * d:IN_CB * d].T.astype(BF16)
        z = _in_projection(hb, w_tok)
        vt = _v_projection_t(hb, w_vt)
        on = _diff_attention(z, vt, lam, bias, far_bias, subln_g[l], batch, 1.0 - lam_init)
        w_router = jnp.zeros((d, ROUTER_LANES), F32)
        w_router = w_router.at[:, :N_GROUPS].set(w_rg[l]).at[:, N_GROUPS:N_GROUPS + N_EXPERTS].set(w_re[l])
        b_router = jnp.zeros((1, ROUTER_LANES), F32)
        b_router = b_router.at[0, :N_GROUPS].set(b_rg[l]).at[0, N_GROUPS:N_GROUPS + N_EXPERTS].set(b_re[l])
        h1, h1b, logits = _mixer_tail(
            on, z, h, w_attn_proj[l].astype(BF16), w_conv_proj[l].astype(BF16), w_out[l].astype(BF16),
            conv_w[l], gate_b[l], ln1_g[l], ln1_b[l], w_router.astype(BF16), b_router, seq, alpha)

        gate, dest, src_tok, blk_e, n_used = _route(logits, n_blk)
        xbuf = jnp.take(h1b, src_tok, axis=0, mode="clip")
        ybuf = _expert_mlp(xbuf, blk_e, n_used, w1, w3, w2, l)
        y_both = jnp.take(ybuf, dest, axis=0, mode="clip")
        h, hb = _combine(h1, y_both, gate, ln2_g[l], ln2_b[l], alpha)
    return h.reshape(batch, seq, d)
```

```python
import functools
import math

import jax
import jax.numpy as jnp
from jax import lax
from jax.experimental import pallas as pl
from jax.experimental.pallas import tpu as pltpu

F32 = jnp.float32
BF16 = jnp.bfloat16

N_HEADS = 8
HEAD_DIM = 64
V_DIM = 2 * HEAD_DIM
N_BUCKETS = 32
MAX_DISTANCE = 128
CONV_K = 3
N_GROUPS = 4
EXPERTS_PER_GROUP = 8
N_EXPERTS = N_GROUPS * EXPERTS_PER_GROUP
TOP_K = 2
LN_EPS = 1e-5
MASK_VALUE = -1e30
LOG2_E = math.log2(math.e)

IN_Q, IN_K, IN_V, IN_CB, IN_CC, IN_CX, IN_GA, IN_GC = range(8)
COL_Q, COL_K, COL_CB, COL_CC, COL_CX, COL_GA, COL_GC = range(7)

LANES = 128
BF16_SUBLANES = 16
ROUTER_LANES = LANES

LN_ROWS = 512
PROJ_TM = 1024
PROJ_TN = 1024
ATTN_T = 1024
EXPERT_ROWS = 512
VMEM_LIMIT = 56 * 1024 * 1024


def _cparams(n_axes, flags=None):
    return pltpu.CompilerParams(dimension_semantics=("arbitrary",) * n_axes,
                                vmem_limit_bytes=VMEM_LIMIT, flags=flags)


def _layer_norm(x, g, b):
    mu = jnp.mean(x, axis=-1, keepdims=True)
    xc = x - mu
    var = jnp.mean(xc * xc, axis=-1, keepdims=True)
    return xc * lax.rsqrt(var + LN_EPS) * g + b


def _ln_kernel(x_ref, g_ref, b_ref, h_ref, hb_ref):
    h = _layer_norm(x_ref[...], g_ref[...], b_ref[...])
    h_ref[...] = h
    hb_ref[...] = h.astype(BF16)


def _input_layer_norm(x, g, b):
    n, d = x.shape
    row = pl.BlockSpec((LN_ROWS, d), lambda i: (i, 0))
    vec = pl.BlockSpec((1, d), lambda i: (0, 0))
    return pl.pallas_call(
        _ln_kernel,
        grid=(n // LN_ROWS,),
        in_specs=[row, vec, vec],
        out_specs=[row, row],
        out_shape=[jax.ShapeDtypeStruct((n, d), F32), jax.ShapeDtypeStruct((n, d), BF16)],
        compiler_params=_cparams(1),
        name="input_layer_norm",
    )(x, g.reshape(1, d), b.reshape(1, d))


def _matmul_kernel(x_ref, w_ref, o_ref):
    o_ref[...] = jnp.dot(x_ref[...], w_ref[...], preferred_element_type=F32).astype(o_ref.dtype)


def _in_projection(hb, w):
    n, d = hb.shape
    c = w.shape[1]
    return pl.pallas_call(
        _matmul_kernel,
        grid=(n // PROJ_TM, c // PROJ_TN),
        in_specs=[pl.BlockSpec((PROJ_TM, d), lambda i, j: (i, 0)),
                  pl.BlockSpec((d, PROJ_TN), lambda i, j: (0, j))],
        out_specs=pl.BlockSpec((PROJ_TM, PROJ_TN), lambda i, j: (i, j)),
        out_shape=jax.ShapeDtypeStruct((n, c), BF16),
        compiler_params=_cparams(2),
        name="in_projection",
    )(hb, w)


def _matmul_nt_kernel(w_ref, x_ref, o_ref):
    o_ref[...] = lax.dot_general(w_ref[...], x_ref[...], (((1,), (1,)), ((), ())),
                                 preferred_element_type=F32).astype(o_ref.dtype)


def _v_projection_t(hb, wvt):
    n, d = hb.shape
    c = wvt.shape[0]
    t = ATTN_T
    return pl.pallas_call(
        _matmul_nt_kernel,
        grid=(n // t,),
        in_specs=[pl.BlockSpec((c, d), lambda i: (0, 0)),
                  pl.BlockSpec((t, d), lambda i: (i, 0))],
        out_specs=pl.BlockSpec((None, c, t), lambda i: (i, 0, 0)),
        out_shape=jax.ShapeDtypeStruct((n // t, c, t), BF16),
        compiler_params=_cparams(1),
        name="v_projection_t",
    )(wvt, hb)


def _rel_bucket(rel):
    n = jnp.maximum(rel, 0)
    max_exact = N_BUCKETS // 2
    nf = jnp.maximum(n, 1).astype(F32)
    large = max_exact + (jnp.log(nf / max_exact) / math.log(MAX_DISTANCE / max_exact)
                         * (N_BUCKETS - max_exact)).astype(jnp.int32)
    large = jnp.minimum(large, N_BUCKETS - 1)
    return jnp.where(n < max_exact, n, large)


def _bias_tiles(rel_table, t):
    assert t >= MAX_DISTANCE and LANES + 1 >= MAX_DISTANCE
    rel = jnp.arange(t, dtype=jnp.int32)[None, :] - jnp.arange(t, dtype=jnp.int32)[:, None]
    table = rel_table.astype(F32) * LOG2_E
    far = table[N_BUCKETS - 1]

    def lookup(r):
        onehot = (_rel_bucket(r)[:, :, None] == jnp.arange(N_BUCKETS, dtype=jnp.int32)).astype(F32)
        return jnp.einsum('kqb,bh->hkq', onehot, table, precision=lax.Precision.HIGHEST)

    diag = jnp.where((rel >= 0)[None], lookup(rel), MASK_VALUE)
    prev_fix = lookup(rel[:, :LANES] + t) - far[:, None, None]
    return diag, prev_fix, far


def _attn_kernel(lam_ref, far_ref, q_ref, k_ref, vt_ref, diag_ref, fix_ref, g_ref, o_ref,
                 m_ref, acc_ref, sa_ref, sb_ref, ma_ref, mb_ref, *, t, post_scale):
    head = pl.program_id(0)
    qi = pl.program_id(2)
    q = (q_ref[...].astype(F32) * (HEAD_DIM ** -0.5 * LOG2_E)).astype(BF16)
    lane = lax.broadcasted_iota(jnp.int32, q.shape, 1)
    zero = jnp.zeros_like(q)
    q_halves = (jnp.where(lane < HEAD_DIM, q, zero), jnp.where(lane >= HEAD_DIM, q, zero))
    ones = jnp.ones((BF16_SUBLANES, t), BF16)

    m_ref[...] = jnp.full(m_ref.shape, MASK_VALUE, F32)
    acc_ref[...] = jnp.zeros(acc_ref.shape, F32)

    buf_a = (sa_ref, ma_ref)
    buf_b = (sb_ref, mb_ref)

    def scores(j, buf):
        s_ref, smax_ref = buf
        k = k_ref[pl.ds(pl.multiple_of(j * t, t), t), :]
        for br in range(2):
            s = lax.dot_general(k, q_halves[br], (((1,), (1,)), ((), ())),
                                preferred_element_type=F32)
            s_ref[br] = s
            smax_ref[br] = jnp.max(s, axis=0, keepdims=True)

    def probs(buf, bias, shift):
        s_ref, smax_ref = buf
        out = []
        for br in range(2):
            s = s_ref[br]
            m_prev = m_ref[br]
            if bias is None:
                m_tile = smax_ref[br] + shift
                m_new = jnp.maximum(m_prev, m_tile)
                p = jnp.exp2(s - (m_new - shift)).astype(BF16)
            else:
                s = s + bias
                m_new = jnp.maximum(m_prev, jnp.max(s, axis=0, keepdims=True))
                p = jnp.exp2(s - m_new).astype(BF16)
            out.append((p, jnp.exp2(m_prev - m_new)))
            m_ref[br] = m_new
        return out

    def accumulate(j, tile_probs):
        v_aug = jnp.concatenate([vt_ref[j], ones], axis=0)
        for br, (p, alpha) in enumerate(tile_probs):
            acc_ref[br] = alpha * acc_ref[br] + jnp.dot(v_aug, p, preferred_element_type=F32)

    def fix_previous(buf):
        s_ref, smax_ref = buf
        for br in range(2):
            blk = s_ref[br, :, :LANES] + fix_ref[...]
            s_ref[br, :, :LANES] = blk
            smax_ref[br, :, :LANES] = jnp.max(blk, axis=0, keepdims=True)

    far_bias = far_ref[head]

    def tile_at(r):
        return jnp.where(r == 0, qi - 1, jnp.where(r == qi, qi, r - 1))

    @pl.when(qi >= 1)
    def _():
        odd = qi % 2

        @pl.when(odd == 1)
        def _():
            scores(qi - 1, buf_b)
            fix_previous(buf_b)
            tile_probs = probs(buf_b, None, far_bias)
            scores(tile_at(1), buf_a)
            accumulate(qi - 1, tile_probs)

        @pl.when(odd == 0)
        def _():
            scores(qi - 1, buf_a)
            fix_previous(buf_a)

        def pair(i, carry):
            r = odd + 2 * i
            tile_probs = probs(buf_a, None, far_bias)
            scores(tile_at(r + 1), buf_b)
            accumulate(tile_at(r), tile_probs)
            tile_probs = probs(buf_b, None, far_bias)
            scores(tile_at(r + 2), buf_a)
            accumulate(tile_at(r + 1), tile_probs)
            return carry

        lax.fori_loop(0, (qi - odd) // 2, pair, 0)

    @pl.when(qi == 0)
    def _():
        scores(0, buf_a)

    accumulate(qi, probs(buf_a, diag_ref[...], None))

    a1 = acc_ref[0]
    a2 = acc_ref[1]
    o = a1[:V_DIM] / a1[V_DIM:V_DIM + 1] - lam_ref[0] * (a2[:V_DIM] / a2[V_DIM:V_DIM + 1])
    ms = jnp.mean(o * o, axis=0, keepdims=True)
    on = o * lax.rsqrt(ms + LN_EPS) * (g_ref[...] * post_scale)
    o_ref[...] = on.T.astype(o_ref.dtype)


def _diff_attention(z, vt, lam, diag_bias, prev_fix, far_bias, subln_g, batch, post_scale):
    n = z.shape[0]
    seq = n // batch
    t = ATTN_T
    nq = seq // t
    d_attn = N_HEADS * V_DIM
    smem = pl.BlockSpec(memory_space=pltpu.SMEM)
    once = pl.Buffered(1)
    return pl.pallas_call(
        functools.partial(_attn_kernel, t=t, post_scale=post_scale),
        grid=(N_HEADS, batch, nq),
        in_specs=[
            smem, smem,
            pl.BlockSpec((t, V_DIM), lambda h, b, i: (b * nq + i, COL_Q * N_HEADS + h)),
            pl.BlockSpec((seq, V_DIM), lambda h, b, i: (b, COL_K * N_HEADS + h), pipeline_mode=once),
            pl.BlockSpec((nq, V_DIM, t), lambda h, b, i: (b, h, 0), pipeline_mode=once),
            pl.BlockSpec((None, t, t), lambda h, b, i: (h, 0, 0), pipeline_mode=once),
            pl.BlockSpec((None, t, LANES), lambda h, b, i: (h, 0, 0), pipeline_mode=once),
            pl.BlockSpec((V_DIM, 1), lambda h, b, i: (0, 0)),
        ],
        out_specs=pl.BlockSpec((t, V_DIM), lambda h, b, i: (b * nq + i, h)),
        out_shape=jax.ShapeDtypeStruct((n, d_attn), BF16),
        scratch_shapes=[pltpu.VMEM((2, 1, t), F32),
                        pltpu.VMEM((2, V_DIM + BF16_SUBLANES, t), F32),
                        pltpu.VMEM((2, t, t), F32), pltpu.VMEM((2, t, t), F32),
                        pltpu.VMEM((2, 1, t), F32), pltpu.VMEM((2, 1, t), F32)],
        compiler_params=_cparams(3),
        name="diff_attention",
    )(lam.reshape(1), far_bias, z, z, vt, diag_bias, prev_fix, subln_g.reshape(V_DIM, 1))


def _mixer_tail_kernel(on_ref, cb_ref, cc_ref, cx_ref, ga_ref, gc_ref, cch_ref, cxh_ref, h_ref,
                       wa_ref, wc_ref, wo_ref, convw_ref, gb_ref, lg_ref, lb_ref, wr_ref, br_ref,
                       h1_ref, h1b_ref, logit_ref, *, seq, alpha):
    i = pl.program_id(0)
    tm = on_ref.shape[0]
    attn = jnp.dot(on_ref[...], wa_ref[...], preferred_element_type=F32)

    u = cc_ref[...].astype(F32) * cx_ref[...].astype(F32)
    halo = cch_ref[...].astype(F32) * cxh_ref[...].astype(F32)
    halo = jnp.where((i * tm) % seq == 0, 0.0, halo)
    hm1 = halo[BF16_SUBLANES - 1:BF16_SUBLANES]
    hm2 = halo[BF16_SUBLANES - 2:BF16_SUBLANES - 1]
    row = lax.broadcasted_iota(jnp.int32, u.shape, 0)
    u1 = jnp.where(row == 0, hm1, pltpu.roll(u, 1, axis=0))
    u2 = jnp.where(row == 0, hm2, jnp.where(row == 1, hm1, pltpu.roll(u, 2, axis=0)))
    w = convw_ref[...]
    y = w[2:3] * u + w[1:2] * u1 + w[0:1] * u2
    conv = jnp.dot((cb_ref[...].astype(F32) * y).astype(BF16), wc_ref[...],
                   preferred_element_type=F32)

    gb = gb_ref[...]
    merged = (jax.nn.sigmoid(ga_ref[...].astype(F32) + gb[0:1]) * attn
              + jax.nn.sigmoid(gc_ref[...].astype(F32) + gb[1:2]) * conv)
    mix = jnp.dot(merged.astype(BF16), wo_ref[...], preferred_element_type=F32)
    h1 = _layer_norm(alpha * h_ref[...] + mix, lg_ref[...], lb_ref[...])
    h1b = h1.astype(BF16)
    h1_ref[...] = h1
    h1b_ref[...] = h1b
    logits = jnp.dot(h1b, wr_ref[...], preferred_element_type=F32) + br_ref[...]
    logit_ref[...] = logits.T


def _mixer_tail(on, z, h, wa, wc, wo, conv_w, gate_b, ln_g, ln_b, wr, br, seq, alpha):
    n, d = h.shape
    tm = LN_ROWS
    halo_blocks = tm // BF16_SUBLANES

    def col(c):
        return pl.BlockSpec((tm, d), lambda i: (i, c))

    def halo(c):
        return pl.BlockSpec((BF16_SUBLANES, d), lambda i: (jnp.maximum(i * halo_blocks - 1, 0), c))

    def full(shape):
        return pl.BlockSpec(shape, lambda i: (0,) * len(shape))

    row = pl.BlockSpec((tm, d), lambda i: (i, 0))
    return pl.pallas_call(
        functools.partial(_mixer_tail_kernel, seq=seq, alpha=alpha),
        grid=(n // tm,),
        in_specs=[row, col(COL_CB), col(COL_CC), col(COL_CX), col(COL_GA), col(COL_GC),
                  halo(COL_CC), halo(COL_CX), row,
                  full((d, d)), full((d, d)), full((d, d)), full((CONV_K, d)), full((2, d)),
                  full((1, d)), full((1, d)), full((d, ROUTER_LANES)), full((1, ROUTER_LANES))],
        out_specs=[row, row, pl.BlockSpec((ROUTER_LANES, tm), lambda i: (0, i))],
        out_shape=[jax.ShapeDtypeStruct((n, d), F32), jax.ShapeDtypeStruct((n, d), BF16),
                   jax.ShapeDtypeStruct((ROUTER_LANES, n), F32)],
        compiler_params=_cparams(1),
        name="mixer_tail",
    )(on, z, z, z, z, z, z, z, h, wa, wc, wo, conv_w, gate_b,
      ln_g.reshape(1, d), ln_b.reshape(1, d), wr, br)


def _route(logits_t, n_blk):
    n_tok = logits_t.shape[1]
    n_asg = n_tok * TOP_K
    gl = logits_t[:N_GROUPS]
    g_max = jnp.max(gl, axis=0, keepdims=True)
    g_idx = jnp.argmax(gl, axis=0).astype(jnp.int32)
    g_p = 1.0 / jnp.sum(jnp.exp(gl - g_max), axis=0)
    e_all = logits_t[N_GROUPS:N_GROUPS + N_EXPERTS].reshape(N_GROUPS, EXPERTS_PER_GROUP, n_tok)
    in_group = g_idx[None, None, :] == jnp.arange(N_GROUPS, dtype=jnp.int32)[:, None, None]
    e_logits = jnp.sum(jnp.where(in_group, e_all, 0.0), axis=0)
    loc = jnp.arange(EXPERTS_PER_GROUP, dtype=jnp.int32)[:, None]
    i0 = jnp.argmax(e_logits, axis=0).astype(jnp.int32)
    e0 = jnp.max(e_logits, axis=0)
    rest = jnp.where(loc == i0[None, :], -jnp.inf, e_logits)
    i1 = jnp.argmax(rest, axis=0).astype(jnp.int32)
    e1 = jnp.max(rest, axis=0)
    r = jnp.exp(e1 - e0)
    gate = jnp.stack([g_p / (1.0 + r), g_p * r / (1.0 + r)], axis=1)
    expert = jnp.concatenate([g_idx * EXPERTS_PER_GROUP + i0, g_idx * EXPERTS_PER_GROUP + i1])

    onehot = expert[None, :] == jnp.arange(N_EXPERTS, dtype=jnp.int32)[:, None]
    chunk = EXPERT_ROWS
    triu = jnp.triu(jnp.ones((chunk, chunk), BF16))
    local = jnp.einsum('eck,kj->ecj', onehot.astype(BF16).reshape(N_EXPERTS, n_asg // chunk, chunk),
                       triu, preferred_element_type=F32).astype(jnp.int32)
    totals = local[:, :, -1]
    csum = (local + (jnp.cumsum(totals, axis=1) - totals)[:, :, None]).reshape(N_EXPERTS, n_asg)
    sizes = csum[:, -1]
    padded = (sizes + EXPERT_ROWS - 1) // EXPERT_ROWS * EXPERT_ROWS
    pad_end = jnp.cumsum(padded)
    pad_start = pad_end - padded
    dest = jnp.sum(jnp.where(onehot, csum - 1 + pad_start[:, None], 0), axis=0)
    src_tok = (jnp.arange(n_blk * EXPERT_ROWS, dtype=jnp.int32) % n_tok).at[dest].set(
        jnp.arange(n_asg, dtype=jnp.int32) % n_tok, unique_indices=True, mode="promise_in_bounds")
    blk_row = jnp.arange(n_blk, dtype=jnp.int32) * EXPERT_ROWS
    blk_e = jnp.minimum(jnp.sum((pad_end[None, :] <= blk_row[:, None]).astype(jnp.int32), axis=1),
                        N_EXPERTS - 1)
    n_used = (pad_end[-1] // EXPERT_ROWS).astype(jnp.int32).reshape(1)
    return gate, dest, src_tok, blk_e, n_used


def _expert_kernel(blk_e_ref, n_used_ref, x_ref, w1_ref, w3_ref, w2_ref, y_ref,
                   w1b_ref, w3b_ref, w2b_ref):
    i = pl.program_id(0)
    new_expert = jnp.logical_or(i == 0, blk_e_ref[i] != blk_e_ref[jnp.maximum(i - 1, 0)])

    @pl.when(jnp.logical_and(i < n_used_ref[0], new_expert))
    def _():
        w1b_ref[...] = w1_ref[...].astype(BF16)
        w3b_ref[...] = w3_ref[...].astype(BF16)
        w2b_ref[...] = w2_ref[...].astype(BF16)

    @pl.when(i < n_used_ref[0])
    def _():
        x = x_ref[...]
        a = jnp.dot(x, w1b_ref[...], preferred_element_type=F32)
        b = jnp.dot(x, w3b_ref[...], preferred_element_type=F32)
        mid = (a * jax.nn.sigmoid(a) * b).astype(BF16)
        y_ref[...] = jnp.dot(mid, w2b_ref[...], preferred_element_type=F32).astype(y_ref.dtype)

    @pl.when(i >= n_used_ref[0])
    def _():
        y_ref[...] = jnp.zeros(y_ref.shape, y_ref.dtype)


def _expert_mlp(xbuf, blk_e, n_used, w1, w3, w2, layer):
    n_rows, d = xbuf.shape
    de = w1.shape[-1]
    n_blk = n_rows // EXPERT_ROWS
    grid_spec = pltpu.PrefetchScalarGridSpec(
        num_scalar_prefetch=2,
        grid=(n_blk,),
        in_specs=[
            pl.BlockSpec((EXPERT_ROWS, d), lambda i, be, nu: (jnp.minimum(i, nu[0] - 1), 0)),
            pl.BlockSpec((None, None, d, de), lambda i, be, nu: (layer, be[i], 0, 0)),
            pl.BlockSpec((None, None, d, de), lambda i, be, nu: (layer, be[i], 0, 0)),
            pl.BlockSpec((None, None, de, d), lambda i, be, nu: (layer, be[i], 0, 0)),
        ],
        out_specs=pl.BlockSpec((EXPERT_ROWS, d), lambda i, be, nu: (i, 0)),
        scratch_shapes=[pltpu.VMEM((d, de), BF16), pltpu.VMEM((d, de), BF16),
                        pltpu.VMEM((de, d), BF16)],
    )
    return pl.pallas_call(
        _expert_kernel,
        grid_spec=grid_spec,
        out_shape=jax.ShapeDtypeStruct((n_rows, d), BF16),
        compiler_params=_cparams(1),
        name="expert_mlp",
    )(blk_e, n_used, xbuf, w1, w3, w2)


def _combine_kernel(h_ref, y0_ref, y1_ref, gate_ref, g_ref, b_ref, o_ref, ob_ref, *, alpha):
    gate = gate_ref[...]
    ffn = gate[:, 0:1] * y0_ref[...].astype(F32) + gate[:, 1:2] * y1_ref[...].astype(F32)
    h2 = _layer_norm(alpha * h_ref[...] + ffn, g_ref[...], b_ref[...])
    o_ref[...] = h2
    ob_ref[...] = h2.astype(BF16)


def _combine(h, y_both, gate, ln_g, ln_b, alpha):
    n, d = h.shape
    n_tiles = n // LN_ROWS
    row = pl.BlockSpec((LN_ROWS, d), lambda i: (i, 0))
    vec = pl.BlockSpec((1, d), lambda i: (0, 0))
    return pl.pallas_call(
        functools.partial(_combine_kernel, alpha=alpha),
        grid=(n // LN_ROWS,),
        in_specs=[row, row, pl.BlockSpec((LN_ROWS, d), lambda i: (i + n_tiles, 0)),
                  pl.BlockSpec((LN_ROWS, TOP_K), lambda i: (i, 0)), vec, vec],
        out_specs=[row, row],
        out_shape=[jax.ShapeDtypeStruct((n, d), F32), jax.ShapeDtypeStruct((n, d), BF16)],
        compiler_params=_cparams(1),
        name="moe_combine",
    )(h, y_both, y_both, gate, ln_g.reshape(1, d), ln_b.reshape(1, d))


def kernel(x, ln0_g, ln0_b, rel_table, w_in, gate_b, lam_vecs, subln_g, w_attn_proj, conv_w,
           w_conv_proj, w_out, ln1_g, ln1_b, w_rg, b_rg, w_re, b_re, w1, w3, w2, ln2_g, ln2_b):
    batch, seq, d = x.shape
    depth = w_in.shape[0]
    n_tok = batch * seq
    alpha = (2 * depth) ** 0.25
    assert seq % ATTN_T == 0 and n_tok % PROJ_TM == 0 and seq % LN_ROWS == 0
    n_asg = n_tok * TOP_K
    n_blk = -(-(n_asg + N_EXPERTS * (EXPERT_ROWS - 1)) // EXPERT_ROWS)

    diag_bias, prev_fix, far_bias = _bias_tiles(rel_table, ATTN_T)
    h, hb = _input_layer_norm(x.reshape(n_tok, d), ln0_g, ln0_b)
    for l in range(depth):
        lam_init = 0.8 - 0.6 * math.exp(-0.3 * l)
        lv = lam_vecs[l].astype(F32)
        lam = jnp.exp(jnp.sum(lv[0] * lv[1])) - jnp.exp(jnp.sum(lv[2] * lv[3])) + lam_init

        w_l = w_in[l]
        w_tok = jnp.concatenate([w_l[:, :IN_V * d], w_l[:, IN_CB * d:]], axis=1).astype(BF16)
        w_vt = w_l[:, IN_V * d:IN_CB * d].T.astype(BF16)
        z = _in_projection(hb, w_tok)
        vt = _v_projection_t(hb, w_vt)
        on = _diff_attention(z, vt, lam, diag_bias, prev_fix, far_bias, subln_g[l], batch,
                             1.0 - lam_init)
        w_router = jnp.zeros((d, ROUTER_LANES), F32)
        w_router = w_router.at[:, :N_GROUPS].set(w_rg[l]).at[:, N_GROUPS:N_GROUPS + N_EXPERTS].set(w_re[l])
        b_router = jnp.zeros((1, ROUTER_LANES), F32)
        b_router = b_router.at[0, :N_GROUPS].set(b_rg[l]).at[0, N_GROUPS:N_GROUPS + N_EXPERTS].set(b_re[l])
        h1, h1b, logits = _mixer_tail(
            on, z, h, w_attn_proj[l].astype(BF16), w_conv_proj[l].astype(BF16), w_out[l].astype(BF16),
            conv_w[l], gate_b[l], ln1_g[l], ln1_b[l], w_router.astype(BF16), b_router, seq, alpha)

        gate, dest, src_tok, blk_e, n_used = _route(logits, n_blk)
        xbuf = jnp.take(h1b, src_tok, axis=0, mode="clip")
        ybuf = _expert_mlp(xbuf, blk_e, n_used, w1, w3, w2, l)
        y_both = jnp.take(ybuf, dest, axis=0, mode="clip")
        h, hb = _combine(h1, y_both, gate, ln2_g[l], ln2_b[l], alpha)
    return h.reshape(batch, seq, d)
```
